```python
import jax
import jax.numpy as jnp
from jax import lax
import numpy as np

D_MODEL = 1024
BATCH = 4
SEQ = 8192
DEPTH = 4

N_MEM = 256
X_HEADS = 4
X_HEAD_DIM = 128
X_WIDTH = X_HEADS * X_HEAD_DIM
MLA_HEADS = 12
QK_NOPE = 128
QK_ROPE = 64
V_HEAD = 128
Q_LORA = 384
KV_LORA = 256
ROPE_THETA = 10000.0
MIX_WIDTH = MLA_HEADS * V_HEAD
INNER = MIX_WIDTH + X_WIDTH
RWKV_HEAD = 64
RWKV_HEADS = MIX_WIDTH // RWKV_HEAD
DECAY_LORA = 64
ICLR_LORA = 64
GN_EPS = 64e-5
NORM_EPS = 1e-6
Q_BLOCK = 128
N_MIXERS = 2
N_MLA = (DEPTH + 1) // 2
N_RWKV = DEPTH // 2
MLA_IN = Q_LORA + KV_LORA + QK_ROPE + X_WIDTH + INNER
RWKV_SHIFT = 3 * MIX_WIDTH + DECAY_LORA + ICLR_LORA
RWKV_IN = RWKV_SHIFT + X_WIDTH + INNER

kernel_name = 'hybrid_mla_rwkv7_memxattn_gated'


def split_cols(u, sizes):
    out, start = [], 0
    for n in sizes:
        out.append(u[..., start:start + n])
        start += n
    return out


def rmsnorm(x, g):
    xf = x.astype(jnp.float32)
    y = xf * lax.rsqrt(jnp.mean(xf * xf, axis=-1, keepdims=True) + NORM_EPS)
    return (y * g.astype(jnp.float32)).astype(x.dtype)


def rope_tables(positions):
    inv_freq = ROPE_THETA ** (-jnp.arange(0, QK_ROPE, 2, dtype=jnp.float32) / QK_ROPE)
    ang = positions.astype(jnp.float32)[..., None] * inv_freq
    return jnp.cos(ang), jnp.sin(ang)


def apply_rope(x, cos, sin):
    cos = cos.astype(x.dtype)
    sin = sin.astype(x.dtype)
    x1, x2 = x[..., 0::2], x[..., 1::2]
    return jnp.stack([x1 * cos - x2 * sin, x1 * sin + x2 * cos], axis=-1).reshape(x.shape)


def causal_block_attention(q_nope, q_rope, k_nope, k_rope, v):
    B, S, H, _ = q_nope.shape
    n_blocks = S // Q_BLOCK
    scale = (QK_NOPE + QK_ROPE) ** -0.5
    key_idx = jnp.arange(S)

    def to_blocks(t):
        return jnp.moveaxis(t.reshape(B, n_blocks, Q_BLOCK, *t.shape[2:]), 1, 0)

    def one_block(args):
        qn, qr, blk = args
        s = (jnp.einsum('bqhd,bkhd->bhqk', qn, k_nope)
             + jnp.einsum('bqhr,bkr->bhqk', qr, k_rope)).astype(jnp.float32) * scale
        q_idx = blk * Q_BLOCK + jnp.arange(Q_BLOCK)
        s = jnp.where(key_idx[None, :] <= q_idx[:, None], s, -jnp.inf)
        p = jax.nn.softmax(s, axis=-1).astype(v.dtype)
        return jnp.einsum('bhqk,bkhd->bqhd', p, v)

    out = lax.map(one_block, (to_blocks(q_nope), to_blocks(q_rope), jnp.arange(n_blocks)))
    return jnp.moveaxis(out, 0, 1).reshape(B, S, H * V_HEAD)


def mla_mixer(c_q, c_kv, k_rope_raw, cos, sin, q_norm_g, kv_norm_g, w_uq, w_ukv):
    B, S, _ = c_q.shape
    q = (rmsnorm(c_q, q_norm_g) @ w_uq).reshape(B, S, MLA_HEADS, QK_NOPE + QK_ROPE)
    q_nope = q[..., :QK_NOPE]
    q_rope = apply_rope(q[..., QK_NOPE:], cos[:, :, None, :], sin[:, :, None, :])
    kv = (rmsnorm(c_kv, kv_norm_g) @ w_ukv).reshape(B, S, MLA_HEADS, QK_NOPE + V_HEAD)
    k_nope, v = kv[..., :QK_NOPE], kv[..., QK_NOPE:]
    k_rope = apply_rope(k_rope_raw, cos, sin)
    return causal_block_attention(q_nope, q_rope, k_nope, k_rope, v)


def rwkv7_scan(r, w, k, v, kk, a):
    B, S, H, N = r.shape

    def step(state, inp):
        r_t, w_t, k_t, v_t, kk_t, a_t = inp
        sa = jnp.einsum('bhvk,bhk->bhv', state, -kk_t)
        state = (state * w_t[:, :, None, :]
                 + sa[..., None] * (kk_t * a_t)[:, :, None, :]
                 + v_t[..., None] * k_t[:, :, None, :])
        return state, jnp.einsum('bhvk,bhk->bhv', state, r_t)

    xs = tuple(jnp.moveaxis(t, 1, 0) for t in (r, w, k, v, kk, a))
    _, y = lax.scan(step, jnp.zeros((B, H, N, N), jnp.float32), xs)
    return jnp.moveaxis(y, 0, 1)


def rwkv7_mixer(u_shift, mu, w0, w2, a0, a2, k_k, k_a, r_k, gn_w, gn_b):
    B, S, _ = u_shift.shape
    f32 = jnp.float32
    u = u_shift.astype(f32)
    u_prev = jnp.pad(u[:, :-1], ((0, 0), (1, 0), (0, 0)))
    u = u + (u_prev - u) * mu.astype(f32)
    r, k, v, wd, ad = split_cols(u, (MIX_WIDTH, MIX_WIDTH, MIX_WIDTH, DECAY_LORA, ICLR_LORA))
    w = -jax.nn.softplus(-(w0.astype(f32) + jnp.tanh(wd) @ w2.astype(f32))) - 0.5
    decay = jnp.exp(-jnp.exp(w))
    a = jax.nn.sigmoid(a0.astype(f32) + ad @ a2.astype(f32))

    def heads(t):
        return t.reshape(B, S, RWKV_HEADS, RWKV_HEAD)

    kk = heads(k * k_k.astype(f32))
    kk = kk / jnp.maximum(jnp.linalg.norm(kk, axis=-1, keepdims=True), 1e-12)
    k = k * (1.0 + (a - 1.0) * k_a.astype(f32))
    r, decay, k, v, a = heads(r), heads(decay), heads(k), heads(v), heads(a)
    y = rwkv7_scan(r, decay, k, v, kk, a)
    mean = jnp.mean(y, axis=-1, keepdims=True)
    var = jnp.mean(jnp.square(y - mean), axis=-1, keepdims=True)
    y = (y - mean) * lax.rsqrt(var + GN_EPS)
    y = y * gn_w.astype(f32).reshape(RWKV_HEADS, RWKV_HEAD) + gn_b.astype(f32).reshape(RWKV_HEADS, RWKV_HEAD)
    bonus = jnp.sum(r * k * r_k.astype(f32).reshape(RWKV_HEADS, RWKV_HEAD), axis=-1, keepdims=True) * v
    return (y + bonus).reshape(B, S, MIX_WIDTH).astype(u_shift.dtype)


def memory_attention(q, mem_k, mem_v):
    B, S = q.shape[:2]
    s = jnp.einsum('bshd,bmhd->bhsm', q, mem_k).astype(jnp.float32) * X_HEAD_DIM ** -0.5
    p = jax.nn.softmax(s, axis=-1).astype(mem_v.dtype)
    return jnp.einsum('bhsm,bmhd->bshd', p, mem_v).reshape(B, S, X_WIDTH)


def setup_inputs(seed: int = 0) -> dict:
    key = jax.random.key(seed)
    ks = iter(jax.random.split(key, 32))

    def nrm(shape, scale):
        return scale * jax.random.normal(next(ks), shape, jnp.float32)

    def gain(shape):
        return 1.0 + nrm(shape, 0.02)

    def unif(shape, lo, hi):
        return jax.random.uniform(next(ks), shape, jnp.float32, lo, hi)

    x = nrm((BATCH, SEQ, D_MODEL), 1.0)
    mem = nrm((BATCH, N_MEM, D_MODEL), 1.0)
    offset = jax.random.randint(next(ks), (BATCH, 1), 0, 4096, dtype=jnp.int32)
    positions = (offset + jnp.arange(SEQ, dtype=jnp.int32)[None, :]).astype(jnp.int32)
    return {
        'x': x,
        'mem': mem,
        'positions': positions,
        'norm_g': gain((DEPTH, D_MODEL)),
        'mem_norm_g': gain((DEPTH, D_MODEL)),
        'w_mem_kv': nrm((DEPTH, D_MODEL, 2 * X_WIDTH), D_MODEL ** -0.5),
        'w_in_mla': nrm((N_MLA, D_MODEL, MLA_IN), D_MODEL ** -0.5),
        'mla_q_norm_g': gain((N_MLA, Q_LORA)),
        'mla_kv_norm_g': gain((N_MLA, KV_LORA)),
        'mla_w_uq': nrm((N_MLA, Q_LORA, MLA_HEADS * (QK_NOPE + QK_ROPE)), Q_LORA ** -0.5),
        'mla_w_ukv': nrm((N_MLA, KV_LORA, MLA_HEADS * (QK_NOPE + V_HEAD)), KV_LORA ** -0.5),
        'w_in_rwkv': nrm((N_RWKV, D_MODEL, RWKV_IN), D_MODEL ** -0.5),
        'rwkv_mu': unif((N_RWKV, RWKV_SHIFT), 0.0, 1.0),
        'rwkv_w0': unif((N_RWKV, MIX_WIDTH), -6.0, -1.0),
        'rwkv_w2': nrm((N_RWKV, DECAY_LORA, MIX_WIDTH), 0.1 * DECAY_LORA ** -0.5),
        'rwkv_a0': nrm((N_RWKV, MIX_WIDTH), 0.1),
        'rwkv_a2': nrm((N_RWKV, ICLR_LORA, MIX_WIDTH), 0.1 * ICLR_LORA ** -0.5),
        'rwkv_k_k': 0.85 + nrm((N_RWKV, MIX_WIDTH), 0.02),
        'rwkv_k_a': gain((N_RWKV, MIX_WIDTH)),
        'rwkv_r_k': nrm((N_RWKV, MIX_WIDTH), 0.1),
        'rwkv_gn_w': gain((N_RWKV, MIX_WIDTH)),
        'rwkv_gn_b': nrm((N_RWKV, MIX_WIDTH), 0.02),
        'w_out': nrm((DEPTH, INNER, D_MODEL), INNER ** -0.5),
        'final_g': gain((D_MODEL,)),
    }


def reference(x, mem, positions, norm_g, mem_norm_g, w_mem_kv, w_in_mla, mla_q_norm_g,
              mla_kv_norm_g, mla_w_uq, mla_w_ukv, w_in_rwkv, rwkv_mu, rwkv_w0, rwkv_w2,
              rwkv_a0, rwkv_a2, rwkv_k_k, rwkv_k_a, rwkv_r_k, rwkv_gn_w, rwkv_gn_b,
              w_out, final_g):
    B, S, _ = x.shape
    M = mem.shape[1]
    cos, sin = rope_tables(positions)
    for i in range(DEPTH):
        h = rmsnorm(x, norm_g[i])
        m = rmsnorm(mem, mem_norm_g[i])
        mem_k, mem_v = split_cols(m @ w_mem_kv[i], (X_WIDTH, X_WIDTH))
        mem_k = mem_k.reshape(B, M, X_HEADS, X_HEAD_DIM)
        mem_v = mem_v.reshape(B, M, X_HEADS, X_HEAD_DIM)
        j = i // N_MIXERS
        if i % N_MIXERS == 0:
            u = h @ w_in_mla[j]
            c_q, c_kv, k_rope_raw, q_mem, gate = split_cols(
                u, (Q_LORA, KV_LORA, QK_ROPE, X_WIDTH, INNER))
            mix = mla_mixer(c_q, c_kv, k_rope_raw, cos, sin, mla_q_norm_g[j],
                            mla_kv_norm_g[j], mla_w_uq[j], mla_w_ukv[j])
        else:
            u = h @ w_in_rwkv[j]
            u_shift, q_mem, gate = split_cols(u, (RWKV_SHIFT, X_WIDTH, INNER))
            mix = rwkv7_mixer(u_shift, rwkv_mu[j], rwkv_w0[j], rwkv_w2[j], rwkv_a0[j],
                              rwkv_a2[j], rwkv_k_k[j], rwkv_k_a[j], rwkv_r_k[j],
                              rwkv_gn_w[j], rwkv_gn_b[j])
        mem_out = memory_attention(q_mem.reshape(B, S, X_HEADS, X_HEAD_DIM), mem_k, mem_v)
        y = jnp.concatenate([mix, mem_out], axis=-1) * jax.nn.silu(gate)
        x = x + y @ w_out[i]
    return rmsnorm(x, final_g)
```

```python
import functools
import math

import jax
import jax.numpy as jnp
import numpy as np
from jax import lax
from jax.experimental import pallas as pl
from jax.experimental.pallas import tpu as pltpu

F32 = jnp.float32
BF16 = jnp.bfloat16

D_MODEL = 1024
X_HEADS = 4
X_HEAD_DIM = 128
X_WIDTH = X_HEADS * X_HEAD_DIM
MLA_HEADS = 12
QK_NOPE = 128
QK_ROPE = 64
V_HEAD = 128
Q_LORA = 384
KV_LORA = 256
ROPE_THETA = 10000.0
MIX_WIDTH = MLA_HEADS * V_HEAD
INNER = MIX_WIDTH + X_WIDTH
RWKV_HEAD = 64
RWKV_HEADS = MIX_WIDTH // RWKV_HEAD
DECAY_LORA = 64
ICLR_LORA = 64
GN_EPS = 64e-5
NORM_EPS = 1e-6

LANES = 128
QK_PACK = 2 * LANES
N_PAIRS = MIX_WIDTH // LANES
CHUNK = 64
VMEM_LIMIT = 56 * 1024 * 1024


def _cparams(sem):
    return pltpu.CompilerParams(dimension_semantics=sem, vmem_limit_bytes=VMEM_LIMIT)


def _bdot(a, b):
    return jnp.dot(a.astype(BF16), b.astype(BF16), preferred_element_type=F32)


def _bdot_nt(a, b):
    return lax.dot_general(a.astype(BF16), b.astype(BF16), (((1,), (1,)), ((), ())),
                           preferred_element_type=F32)


def _bdot_tn(a, b):
    return lax.dot_general(a.astype(BF16), b.astype(BF16), (((0,), (0,)), ((), ())),
                           preferred_element_type=F32)


def _split_dot(x, m_bf16, terms):
    acc = None
    rem = x
    for _ in range(terms):
        piece = rem.astype(BF16)
        part = jnp.dot(piece, m_bf16, preferred_element_type=F32)
        acc = part if acc is None else acc + part
        rem = rem - piece.astype(F32)
    return acc


def _norm_proj_kernel(x_ref, g_ref, *refs, n_out):
    w_refs, o_refs = refs[:n_out], refs[n_out:]
    x = x_ref[...].astype(F32)
    ms = jnp.mean(x * x, axis=-1, keepdims=True)
    xn = (x * lax.rsqrt(ms + NORM_EPS) * g_ref[...]).astype(BF16)
    for w_ref, o_ref in zip(w_refs, o_refs):
        o_ref[...] = jnp.dot(xn, w_ref[...], preferred_element_type=F32).astype(o_ref.dtype)


def _norm_proj(x2d, g, weights, out_dtypes, tm):
    m, k = x2d.shape
    n_out = len(weights)
    in_specs = [pl.BlockSpec((tm, k), lambda i: (i, 0)), pl.BlockSpec((1, k), lambda i: (0, 0))]
    in_specs += [pl.BlockSpec(w.shape, lambda i: (0, 0)) for w in weights]
    out_specs = [pl.BlockSpec((tm, w.shape[1]), lambda i: (i, 0)) for w in weights]
    out_shape = [jax.ShapeDtypeStruct((m, w.shape[1]), dt) for w, dt in zip(weights, out_dtypes)]
    return pl.pallas_call(
        functools.partial(_norm_proj_kernel, n_out=n_out),
        grid=(m // tm,),
        in_specs=in_specs,
        out_specs=out_specs,
        out_shape=out_shape,
        compiler_params=_cparams(("parallel",)),
        name="norm_proj",
    )(x2d, g.reshape(1, k).astype(F32), *weights)


def _rope_tables(pos, invp):
    ang = pos * invp
    lane = lax.broadcasted_iota(jnp.int32, ang.shape, 1)
    first = (lane >= QK_NOPE) & (lane < QK_NOPE + QK_ROPE // 2)
    second = (lane >= QK_NOPE + QK_ROPE // 2) & (lane < QK_NOPE + QK_ROPE)
    cosv = jnp.cos(ang)
    sinv = jnp.sin(ang)
    cos_f = jnp.where(first | second, cosv, 1.0)
    sin_a = jnp.where(first, -sinv, 0.0)
    sin_b = jnp.where(second, sinv, 0.0)
    return cos_f, sin_a, sin_b


def _rope_apply(x, cos_f, sin_a, sin_b):
    w = x.shape[1]
    half = QK_ROPE // 2
    return x * cos_f + pltpu.roll(x, w - half, 1) * sin_a + pltpu.roll(x, half, 1) * sin_b


def _mla_up_kernel(cq_ref, ckv_ref, kr_ref, pos_ref, gq_ref, gkv_ref, wq_ref, wk_ref, wv_ref, invp_ref,
                   q_ref, k_ref, v_ref, *, scale):
    def rms(x, g):
        ms = jnp.mean(x * x, axis=-1, keepdims=True)
        return (x * lax.rsqrt(ms + NORM_EPS) * g).astype(BF16)

    cqn = rms(cq_ref[...], gq_ref[...])
    ckvn = rms(ckv_ref[...], gkv_ref[...])
    cos_f, sin_a, sin_b = _rope_tables(pos_ref[...], invp_ref[...])
    kr = _rope_apply(kr_ref[...], cos_f[:, LANES:], sin_a[:, LANES:], sin_b[:, LANES:]).astype(BF16)
    v_ref[...] = jnp.dot(ckvn, wv_ref[...], preferred_element_type=F32).astype(BF16)
    kn = jnp.dot(ckvn, wk_ref[...], preferred_element_type=F32).astype(BF16)
    for h in range(MLA_HEADS):
        qh = jnp.dot(cqn, wq_ref[:, h * QK_PACK:(h + 1) * QK_PACK], preferred_element_type=F32)
        qh = _rope_apply(qh, cos_f, sin_a, sin_b) * scale
        q_ref[:, h * QK_PACK:(h + 1) * QK_PACK] = qh.astype(BF16)
        k_ref[:, h * QK_PACK:h * QK_PACK + LANES] = kn[:, h * LANES:(h + 1) * LANES]
        k_ref[:, h * QK_PACK + LANES:(h + 1) * QK_PACK] = kr


def _mla_up(cq, ckv, kr, pos, gq, gkv, wq, wk, wv, invp, tm):
    m = cq.shape[0]
    row = lambda w: pl.BlockSpec((tm, w), lambda i: (i, 0))
    full = lambda a: pl.BlockSpec(a.shape, lambda i: (0, 0))
    scale = float(QK_NOPE + QK_ROPE) ** -0.5
    return pl.pallas_call(
        functools.partial(_mla_up_kernel, scale=scale),
        grid=(m // tm,),
        in_specs=[row(Q_LORA), row(KV_LORA), row(LANES), row(1), full(gq), full(gkv), full(wq), full(wk),
                  full(wv), full(invp)],
        out_specs=[row(MLA_HEADS * QK_PACK), row(MLA_HEADS * QK_PACK), row(MIX_WIDTH)],
        out_shape=[jax.ShapeDtypeStruct((m, MLA_HEADS * QK_PACK), BF16),
                   jax.ShapeDtypeStruct((m, MLA_HEADS * QK_PACK), BF16),
                   jax.ShapeDtypeStruct((m, MIX_WIDTH), BF16)],
        compiler_params=_cparams(("parallel",)),
        name="mla_up",
    )(cq, ckv, kr, pos, gq, gkv, wq, wk, wv, invp)


def _flash_kernel(q_ref, k_ref, v_ref, o_ref, m_sc, l_sc, acc_sc, *, tq, tk):
    i = pl.program_id(2)
    j = pl.program_id(3)
    last_j = ((i + 1) * tq - 1) // tk

    @pl.when(j == 0)
    def _():
        m_sc[...] = jnp.full_like(m_sc, -jnp.inf)
        l_sc[...] = jnp.zeros_like(l_sc)
        acc_sc[...] = jnp.zeros_like(acc_sc)

    def step(masked):
        s = lax.dot_general(q_ref[0], k_ref[0], (((1,), (1,)), ((), ())), preferred_element_type=F32)
        if masked:
            row = i * tq + lax.broadcasted_iota(jnp.int32, s.shape, 0)
            col = j * tk + lax.broadcasted_iota(jnp.int32, s.shape, 1)
            s = jnp.where(col <= row, s, -jnp.inf)
        m_prev = m_sc[...]
        m_new = jnp.maximum(m_prev, jnp.max(s, axis=-1, keepdims=True))
        alpha = jnp.exp(m_prev - m_new)
        p = jnp.exp(s - m_new)
        l_sc[...] = alpha * l_sc[...] + jnp.sum(p, axis=-1, keepdims=True)
        acc_sc[...] = alpha * acc_sc[...] + jnp.dot(p.astype(BF16), v_ref[0], preferred_element_type=F32)
        m_sc[...] = m_new

    interior = (j + 1) * tk - 1 <= i * tq

    @pl.when(interior)
    def _():
        step(False)

    @pl.when(jnp.logical_and(jnp.logical_not(interior), j <= last_j))
    def _():
        step(True)

    @pl.when(j == last_j)
    def _():
        o_ref[0] = (acc_sc[...] / l_sc[...]).astype(o_ref.dtype)


def _flash(q, k, v, tq, tk):
    b, s, _ = q.shape
    nq, nk = s // tq, s // tk

    def kv_idx(bi, h, i, j):
        return (bi, jnp.minimum(j, ((i + 1) * tq - 1) // tk), h)

    return pl.pallas_call(
        functools.partial(_flash_kernel, tq=tq, tk=tk),
        grid=(b, MLA_HEADS, nq, nk),
        in_specs=[pl.BlockSpec((1, tq, QK_PACK), lambda bi, h, i, j: (bi, i, h)),
                  pl.BlockSpec((1, tk, QK_PACK), kv_idx),
                  pl.BlockSpec((1, tk, V_HEAD), kv_idx)],
        out_specs=pl.BlockSpec((1, tq, V_HEAD), lambda bi, h, i, j: (bi, i, h)),
        out_shape=jax.ShapeDtypeStruct((b, s, MIX_WIDTH), BF16),
        scratch_shapes=[pltpu.VMEM((tq, 1), F32), pltpu.VMEM((tq, 1), F32), pltpu.VMEM((tq, V_HEAD), F32)],
        compiler_params=_cparams(("parallel", "parallel", "parallel", "arbitrary")),
        name="flash",
    )(q, k, v)


def _seg_sum(x, segm):
    return _split_dot(x, segm, 2)


def _rwkv_prep_kernel(ur_ref, uk_ref, uv_ref, uwa_ref, pr_ref, pk_ref, pv_ref, pwa_ref,
                      mur_ref, muk_ref, muv_ref, muwa_ref, w0_ref, a0_ref, wl_ref, kkw_ref, kaw_ref, rkw_ref,
                      lmat_ref, segm_ref,
                      r_o, lw_o, g_o, k_o, v_o, kk_o, bb_o, bonus_o, *, prev_rows):
    first = pl.program_id(1) == 0

    def mix(cur_ref, prev_ref, mu_ref):
        u = cur_ref[0].astype(F32)
        last = prev_ref[0][prev_rows - 1:prev_rows, :].astype(F32)
        last = jnp.where(first, 0.0, last)
        rid = lax.broadcasted_iota(jnp.int32, u.shape, 0)
        up = jnp.where(rid == 0, last, pltpu.roll(u, 1, 0))
        return u + (up - u) * mu_ref[...]

    xr = mix(ur_ref, pr_ref, mur_ref)
    xk = mix(uk_ref, pk_ref, muk_ref)
    xv = mix(uv_ref, pv_ref, muv_ref)
    xwa = mix(uwa_ref, pwa_ref, muwa_ref)

    lane = lax.broadcasted_iota(jnp.int32, xwa.shape, 1)
    z = jnp.where(lane < DECAY_LORA, jnp.tanh(xwa), xwa)
    lo = jnp.dot(z.astype(BF16), wl_ref[...], preferred_element_type=F32)
    zw = -(w0_ref[...] + lo[:, :MIX_WIDTH])
    softplus = jnp.maximum(zw, 0.0) + jnp.log1p(jnp.exp(-jnp.abs(zw)))
    lw = -jnp.exp(-softplus - 0.5)
    a = 1.0 / (1.0 + jnp.exp(-(a0_ref[...] + lo[:, MIX_WIDTH:])))
    kkr = xk * kkw_ref[...]
    kmod = xk * (1.0 + (a - 1.0) * kaw_ref[...])
    rkk = xr * kmod * rkw_ref[...]
    lmat = lmat_ref[...]
    segm = segm_ref[...]
    for p in range(N_PAIRS):
        sl = slice(p * LANES, (p + 1) * LANES)
        n2 = _seg_sum(kkr[:, sl] * kkr[:, sl], segm)
        kk = kkr[:, sl] / jnp.maximum(jnp.sqrt(n2), 1e-12)
        r_o[0, p] = xr[:, sl]
        lw_o[0, p] = lw[:, sl]
        g_o[0, p] = _split_dot_left(lmat, lw[:, sl])
        k_o[0, p] = kmod[:, sl]
        v_o[0, p] = xv[:, sl]
        kk_o[0, p] = kk
        bb_o[0, p] = kk * a[:, sl]
        bonus_o[0, p] = _seg_sum(rkk[:, sl], segm) * xv[:, sl]


def _split_dot_left(m_bf16, x):
    acc = None
    rem = x
    for _ in range(3):
        piece = rem.astype(BF16)
        part = jnp.dot(m_bf16, piece, preferred_element_type=F32)
        acc = part if acc is None else acc + part
        rem = rem - piece.astype(F32)
    return acc


def _rwkv_prep(ur, uk, uv, uwa, mur, muk, muv, muwa, w0, a0, wl, kkw, kaw, rkw, lmat, segm, ts):
    b, s, _ = ur.shape
    prev_rows = 16
    cur = lambda w: pl.BlockSpec((1, ts, w), lambda bi, i: (bi, i, 0))
    prev = lambda w: pl.BlockSpec((1, prev_rows, w),
                                  lambda bi, i: (bi, jnp.maximum(i * (ts // prev_rows) - 1, 0), 0))
    full = lambda a: pl.BlockSpec(a.shape, lambda bi, i: (0,) * a.ndim)
    pm = pl.BlockSpec((1, N_PAIRS, ts, LANES), lambda bi, i: (bi, 0, i, 0))
    small = [mur, muk, muv, muwa, w0, a0, wl, kkw, kaw, rkw, lmat, segm]
    return pl.pallas_call(
        functools.partial(_rwkv_prep_kernel, prev_rows=prev_rows),
        grid=(b, s // ts),
        in_specs=[cur(MIX_WIDTH), cur(MIX_WIDTH), cur(MIX_WIDTH), cur(LANES),
                  prev(MIX_WIDTH), prev(MIX_WIDTH), prev(MIX_WIDTH), prev(LANES)] + [full(a) for a in small],
        out_specs=[pm] * 8,
        out_shape=[jax.ShapeDtypeStruct((b, N_PAIRS, s, LANES), F32)] * 8,
        compiler_params=_cparams(("parallel", "parallel")),
        name="rwkv_prep",
    )(ur, uk, uv, uwa, ur, uk, uv, uwa, *small)


def _rwkv_scan_kernel(r_ref, lw_ref, g_ref, k_ref, v_ref, kk_ref, bb_ref, bonus_ref, gnw_ref, gnb_ref, segm_ref,
                      o_ref, s_ref):
    c2 = 2 * CHUNK

    @pl.when(pl.program_id(1) == 0)
    def _():
        s_ref[...] = jnp.zeros_like(s_ref)

    lane = lax.broadcasted_iota(jnp.int32, (CHUNK, LANES), 1)
    head0 = lane < RWKV_HEAD

    def stack(x):
        return jnp.concatenate([jnp.where(head0, x, 0.0), jnp.where(head0, 0.0, x)], axis=0)

    row = lax.broadcasted_iota(jnp.int32, (2 * c2, 2 * c2), 0)
    col = lax.broadcasted_iota(jnp.int32, (2 * c2, 2 * c2), 1)
    same_head = ((row // CHUNK) % 2) == ((col // CHUNK) % 2)
    tr, tc = row % CHUNK, col % CHUNK
    aa_mask = same_head & ((tc < tr) | ((row >= c2) & (tc == tr)))
    segm = segm_ref[...]

    for p in range(N_PAIRS):
        r, lw, g = r_ref[0, p], lw_ref[0, p], g_ref[0, p]
        k, v, kk, bb = k_ref[0, p], v_ref[0, p], kk_ref[0, p], bb_ref[0, p]
        s0 = s_ref[p]
        g_last = g[CHUNK - 1:CHUNK, :]
        e_neg = jnp.exp(-g)
        e_rel = jnp.exp(g_last - g)
        ats = stack(-kk * jnp.exp(g - lw))
        rts = stack(r * jnp.exp(g))
        vs = stack(v)
        bt, kt = bb * e_neg, k * e_neg
        aa = _bdot_nt(jnp.concatenate([ats, rts], axis=0), jnp.concatenate([bt, bt, kt, kt], axis=0))
        aa = jnp.where(aa_mask, aa, 0.0)
        x = aa[:c2, :c2]
        z = jnp.concatenate([ats, _bdot(aa[:c2, c2:], vs)], axis=1)
        steps = int(math.log2(CHUNK))
        for lvl in range(steps):
            z = z + _bdot(x, z)
            if lvl + 1 < steps:
                x = _bdot(x, x)
        us = _bdot_nt(z[:, :LANES], s0) + z[:, LANES:]
        ys = _bdot_nt(rts, s0) + _bdot(aa[c2:, :c2], us) + _bdot(aa[c2:, c2:], vs)
        y = ys[:CHUNK] + ys[CHUNK:]
        s_ref[p] = s0 * jnp.exp(g_last) + _bdot_tn(jnp.concatenate([us, vs], axis=0),
                                                    jnp.concatenate([stack(bb * e_rel), stack(k * e_rel)], axis=0))
        mean = _seg_sum(y, segm) * (1.0 / RWKV_HEAD)
        d = y - mean
        var = _seg_sum(d * d, segm) * (1.0 / RWKV_HEAD)
        out = d * lax.rsqrt(var + GN_EPS) * gnw_ref[p] + gnb_ref[p] + bonus_ref[0, p]
        o_ref[0, :, p * LANES:(p + 1) * LANES] = out.astype(o_ref.dtype)


def _rwkv_scan(r, lw, g, k, v, kk, bb, bonus, gnw, gnb, segm):
    b, _, s, _ = r.shape
    pm = pl.BlockSpec((1, N_PAIRS, CHUNK, LANES), lambda bi, c: (bi, 0, c, 0))
    full = lambda a: pl.BlockSpec(a.shape, lambda bi, c: (0,) * a.ndim)
    return pl.pallas_call(
        _rwkv_scan_kernel,
        grid=(b, s // CHUNK),
        in_specs=[pm] * 8 + [full(gnw), full(gnb), full(segm)],
        out_specs=pl.BlockSpec((1, CHUNK, MIX_WIDTH), lambda bi, c: (bi, c, 0)),
        out_shape=jax.ShapeDtypeStruct((b, s, MIX_WIDTH), BF16),
        scratch_shapes=[pltpu.VMEM((N_PAIRS, LANES, LANES), F32)],
        compiler_params=_cparams(("parallel", "arbitrary")),
        name="rwkv_scan",
    )(r, lw, g, k, v, kk, bb, bonus, gnw, gnb, segm)


def _out_kernel(x_ref, mix_ref, qm_ref, gate_ref, mk_ref, mv_ref, wo_ref, o_ref):
    gate = gate_ref[0].astype(F32)
    sg = gate * (1.0 / (1.0 + jnp.exp(-gate)))
    acc = x_ref[0] + _bdot(mix_ref[0].astype(F32) * sg[:, :MIX_WIDTH], wo_ref[:MIX_WIDTH, :])
    heads = []
    for h in range(X_HEADS):
        sl = slice(h * X_HEAD_DIM, (h + 1) * X_HEAD_DIM)
        s = lax.dot_general(qm_ref[0][:, sl], mk_ref[0][:, sl], (((1,), (1,)), ((), ())),
                            preferred_element_type=F32) * (X_HEAD_DIM ** -0.5)
        p = jnp.exp(s - jnp.max(s, axis=-1, keepdims=True))
        p = p / jnp.sum(p, axis=-1, keepdims=True)
        heads.append(jnp.dot(p.astype(BF16), mv_ref[0][:, sl], preferred_element_type=F32))
    mem_out = jnp.concatenate(heads, axis=1)
    acc = acc + _bdot(mem_out * sg[:, MIX_WIDTH:], wo_ref[MIX_WIDTH:, :])
    o_ref[0] = acc


def _out(x, mix, qm, gate, mk, mv, wo, ts):
    b, s, d = x.shape
    n_mem = mk.shape[1]
    tok = lambda w: pl.BlockSpec((1, ts, w), lambda bi, i: (bi, i, 0))
    memspec = pl.BlockSpec((1, n_mem, X_WIDTH), lambda bi, i: (bi, 0, 0))
    return pl.pallas_call(
        _out_kernel,
        grid=(b, s // ts),
        in_specs=[tok(d), tok(MIX_WIDTH), tok(X_WIDTH), tok(INNER), memspec, memspec,
                  pl.BlockSpec(wo.shape, lambda bi, i: (0, 0))],
        out_specs=tok(d),
        out_shape=jax.ShapeDtypeStruct((b, s, d), F32),
        compiler_params=_cparams(("parallel", "parallel")),
        name="out_proj",
    )(x, mix, qm, gate, mk, mv, wo)


def _final_norm_kernel(x_ref, g_ref, o_ref):
    x = x_ref[...]
    ms = jnp.mean(x * x, axis=-1, keepdims=True)
    o_ref[...] = x * lax.rsqrt(ms + NORM_EPS) * g_ref[...]


def _final_norm(x2d, g, tm):
    m, d = x2d.shape
    return pl.pallas_call(
        _final_norm_kernel,
        grid=(m // tm,),
        in_specs=[pl.BlockSpec((tm, d), lambda i: (i, 0)), pl.BlockSpec((1, d), lambda i: (0, 0))],
        out_specs=pl.BlockSpec((tm, d), lambda i: (i, 0)),
        out_shape=jax.ShapeDtypeStruct((m, d), F32),
        compiler_params=_cparams(("parallel",)),
        name="final_norm",
    )(x2d, g.reshape(1, d))


def _rope_perm():
    return np.concatenate([np.arange(0, QK_ROPE, 2), np.arange(1, QK_ROPE, 2)])


def _pack_mla_weights(w_in, w_uq, w_ukv):
    cq_w, ckv_w, kr_w, qm_w, gate_w = jnp.split(
        w_in, np.cumsum([Q_LORA, KV_LORA, QK_ROPE, X_WIDTH]).tolist(), axis=1)
    perm = _rope_perm()
    kr_w = jnp.pad(kr_w[:, perm], ((0, 0), (0, LANES - QK_ROPE)))
    wq = w_uq.reshape(Q_LORA, MLA_HEADS, QK_NOPE + QK_ROPE)
    wq = jnp.concatenate([wq[:, :, :QK_NOPE], wq[:, :, QK_NOPE:][:, :, perm],
                          jnp.zeros((Q_LORA, MLA_HEADS, QK_PACK - QK_NOPE - QK_ROPE), w_uq.dtype)], axis=2)
    wq = wq.reshape(Q_LORA, MLA_HEADS * QK_PACK)
    wkv = w_ukv.reshape(KV_LORA, MLA_HEADS, QK_NOPE + V_HEAD)
    wk = wkv[:, :, :QK_NOPE].reshape(KV_LORA, MLA_HEADS * QK_NOPE)
    wv = wkv[:, :, QK_NOPE:].reshape(KV_LORA, MLA_HEADS * V_HEAD)
    bf = lambda a: a.astype(BF16)
    return [bf(cq_w), bf(ckv_w), bf(kr_w), bf(qm_w), bf(gate_w)], bf(wq), bf(wk), bf(wv)


def _rope_lane_freq():
    inv_freq = ROPE_THETA ** (-jnp.arange(0, QK_ROPE, 2, dtype=F32) / QK_ROPE)
    pad = jnp.zeros((QK_PACK - QK_NOPE - QK_ROPE,), F32)
    return jnp.concatenate([jnp.zeros((QK_NOPE,), F32), inv_freq, inv_freq, pad]).reshape(1, QK_PACK)


def _chunk_tri(ts):
    t = np.arange(ts)
    return jnp.asarray((t[:, None] // CHUNK == t[None, :] // CHUNK) & (t[None, :] <= t[:, None]), BF16)


def _head_seg():
    t = np.arange(LANES)
    return jnp.asarray(t[:, None] // RWKV_HEAD == t[None, :] // RWKV_HEAD, BF16)


def kernel(x, mem, positions, norm_g, mem_norm_g, w_mem_kv, w_in_mla, mla_q_norm_g, mla_kv_norm_g, mla_w_uq,
           mla_w_ukv, w_in_rwkv, rwkv_mu, rwkv_w0, rwkv_w2, rwkv_a0, rwkv_a2, rwkv_k_k, rwkv_k_a, rwkv_r_k,
           rwkv_gn_w, rwkv_gn_b, w_out, final_g):
    b, s, d = x.shape
    n_mem = mem.shape[1]
    depth = norm_g.shape[0]
    t = b * s
    tm = min(512, s)
    pos = positions.reshape(t, 1).astype(F32)
    invp = _rope_lane_freq()
    segm = _head_seg()
    prep_ts = min(128, s)
    lmat = _chunk_tri(prep_ts)
    row = lambda a: a.reshape(1, -1).astype(F32)

    for i in range(depth):
        j = i // 2
        mk, mv = _norm_proj(mem.reshape(b * n_mem, d), mem_norm_g[i],
                            [w_mem_kv[i][:, :X_WIDTH].astype(BF16), w_mem_kv[i][:, X_WIDTH:].astype(BF16)],
                            [BF16, BF16], tm=min(256, b * n_mem))
        mk = mk.reshape(b, n_mem, X_WIDTH)
        mv = mv.reshape(b, n_mem, X_WIDTH)
        if i % 2 == 0:
            w_list, wq, wk, wv = _pack_mla_weights(w_in_mla[j], mla_w_uq[j], mla_w_ukv[j])
            cq, ckv, kr, qm, gate = _norm_proj(x.reshape(t, d), norm_g[i], w_list, [F32, F32, F32, BF16, BF16],
                                               tm=tm)
            q, k, v = _mla_up(cq, ckv, kr, pos, row(mla_q_norm_g[j]), row(mla_kv_norm_g[j]), wq, wk, wv, invp,
                              tm=min(256, s))
            tq = min(512, s)
            mix = _flash(q.reshape(b, s, -1), k.reshape(b, s, -1), v.reshape(b, s, -1), tq, tq)
        else:
            w = w_in_rwkv[j]
            edges = np.cumsum([MIX_WIDTH, MIX_WIDTH, MIX_WIDTH, DECAY_LORA + ICLR_LORA, X_WIDTH]).tolist()
            w_list = [a.astype(BF16) for a in jnp.split(w, edges, axis=1)]
            ur, uk, uv, uwa, qm, gate = _norm_proj(x.reshape(t, d), norm_g[i], w_list, [BF16] * 6, tm=min(256, s))
            mu = rwkv_mu[j]
            mur, muk, muv, muwa = [row(a) for a in jnp.split(mu, edges[:3])]
            zeros = jnp.zeros((DECAY_LORA, MIX_WIDTH), F32)
            wl = jnp.concatenate([jnp.concatenate([rwkv_w2[j], zeros], axis=1),
                                  jnp.concatenate([zeros, rwkv_a2[j]], axis=1)], axis=0).astype(BF16)
            sh = lambda a: a.reshape(b, s, -1)
            r, lw, g, kmod, vv, kk, bb, bonus = _rwkv_prep(
                sh(ur), sh(uk), sh(uv), sh(uwa), mur, muk, muv, muwa, row(rwkv_w0[j]), row(rwkv_a0[j]), wl,
                row(rwkv_k_k[j]), row(rwkv_k_a[j]), row(rwkv_r_k[j]), lmat, segm, ts=prep_ts)
            gnw = rwkv_gn_w[j].reshape(N_PAIRS, 1, LANES)
            gnb = rwkv_gn_b[j].reshape(N_PAIRS, 1, LANES)
            mix = _rwkv_scan(r, lw, g, kmod, vv, kk, bb, bonus, gnw, gnb, segm)
        x = _out(x, mix, qm.reshape(b, s, -1), gate.reshape(b, s, -1), mk, mv, w_out[i].astype(BF16),
                 ts=min(512, s))
    return _final_norm(x.reshape(t, d), final_g, tm=tm).reshape(b, s, d)
```

```python
import functools
import math

import jax
import jax.numpy as jnp
import numpy as np
from jax import lax
from jax.experimental import pallas as pl
from jax.experimental.pallas import tpu as pltpu

F32 = jnp.float32
BF16 = jnp.bfloat16

D_MODEL = 1024
X_HEADS = 4
X_HEAD_DIM = 128
X_WIDTH = X_HEADS * X_HEAD_DIM
MLA_HEADS = 12
QK_NOPE = 128
QK_ROPE = 64
V_HEAD = 128
Q_LORA = 384
KV_LORA = 256
ROPE_THETA = 10000.0
MIX_WIDTH = MLA_HEADS * V_HEAD
INNER = MIX_WIDTH + X_WIDTH
RWKV_HEAD = 64
RWKV_HEADS = MIX_WIDTH // RWKV_HEAD
DECAY_LORA = 64
ICLR_LORA = 64
GN_EPS = 64e-5
NORM_EPS = 1e-6

LANES = 128
QK_PACK = 2 * LANES
N_PAIRS = MIX_WIDTH // LANES
CHUNK = 64
FLASH_TILE = 512
VMEM_LIMIT = 56 * 1024 * 1024


def _cparams(sem):
    return pltpu.CompilerParams(dimension_semantics=sem, vmem_limit_bytes=VMEM_LIMIT)


def _bdot(a, b):
    return jnp.dot(a.astype(BF16), b.astype(BF16), preferred_element_type=F32)


def _bdot_nt(a, b):
    return lax.dot_general(a.astype(BF16), b.astype(BF16), (((1,), (1,)), ((), ())),
                           preferred_element_type=F32)


def _bdot_tn(a, b):
    return lax.dot_general(a.astype(BF16), b.astype(BF16), (((0,), (0,)), ((), ())),
                           preferred_element_type=F32)


def _split_dot(x, m_bf16, terms):
    acc = None
    rem = x
    for _ in range(terms):
        piece = rem.astype(BF16)
        part = jnp.dot(piece, m_bf16, preferred_element_type=F32)
        acc = part if acc is None else acc + part
        rem = rem - piece.astype(F32)
    return acc


def _norm_proj_kernel(x_ref, g_ref, *refs, n_out):
    w_refs, o_refs = refs[:n_out], refs[n_out:]
    x = x_ref[...].astype(F32)
    ms = jnp.mean(x * x, axis=-1, keepdims=True)
    xn = (x * lax.rsqrt(ms + NORM_EPS) * g_ref[...]).astype(BF16)
    for w_ref, o_ref in zip(w_refs, o_refs):
        o_ref[...] = jnp.dot(xn, w_ref[...], preferred_element_type=F32).astype(o_ref.dtype)


def _norm_proj(x2d, g, weights, out_dtypes, tm):
    m, k = x2d.shape
    n_out = len(weights)
    in_specs = [pl.BlockSpec((tm, k), lambda i: (i, 0)), pl.BlockSpec((1, k), lambda i: (0, 0))]
    in_specs += [pl.BlockSpec(w.shape, lambda i: (0, 0)) for w in weights]
    out_specs = [pl.BlockSpec((tm, w.shape[1]), lambda i: (i, 0)) for w in weights]
    out_shape = [jax.ShapeDtypeStruct((m, w.shape[1]), dt) for w, dt in zip(weights, out_dtypes)]
    return pl.pallas_call(
        functools.partial(_norm_proj_kernel, n_out=n_out),
        grid=(m // tm,),
        in_specs=in_specs,
        out_specs=out_specs,
        out_shape=out_shape,
        compiler_params=_cparams(("parallel",)),
        name="norm_proj",
    )(x2d, g.reshape(1, k).astype(F32), *weights)


HALF_ROPE = QK_ROPE // 2


def _mla_up_kernel(cq_ref, ckv_ref, kr_ref, posc_ref, posr_ref, gq_ref, gkv_ref, wqt_ref, wk_ref, wvt_ref,
                   invr_ref, invc_ref, qt_ref, k_ref, vt_ref, *, scale):
    def rms(x, g):
        ms = jnp.mean(x * x, axis=-1, keepdims=True)
        return (x * lax.rsqrt(ms + NORM_EPS) * g).astype(BF16)

    cqn = rms(cq_ref[0], gq_ref[...])
    ckvn = rms(ckv_ref[0], gkv_ref[...])

    ang = posc_ref[0] * invr_ref[...]
    lane = lax.broadcasted_iota(jnp.int32, ang.shape, 1)
    cosv, sinv = jnp.cos(ang), jnp.sin(ang)
    kr = kr_ref[0]
    kr = (kr * cosv
          + pltpu.roll(kr, LANES - HALF_ROPE, 1) * jnp.where(lane < HALF_ROPE, -sinv, 0.0)
          + pltpu.roll(kr, HALF_ROPE, 1) * jnp.where((lane >= HALF_ROPE) & (lane < QK_ROPE), sinv, 0.0))
    kr = kr.astype(BF16)
    kn = jnp.dot(ckvn, wk_ref[...], preferred_element_type=F32).astype(BF16)

    ang_t = invc_ref[...] * posr_ref[0]
    cos_t, sin_t = jnp.cos(ang_t), jnp.sin(ang_t)
    r0, r1, r2 = QK_NOPE, QK_NOPE + HALF_ROPE, QK_NOPE + QK_ROPE
    for h in range(MLA_HEADS):
        k_ref[0, h, :, :LANES] = kn[:, h * LANES:(h + 1) * LANES]
        k_ref[0, h, :, LANES:] = kr
        vt_ref[0, h, 0] = _bdot_nt(wvt_ref[h * V_HEAD:(h + 1) * V_HEAD, :], ckvn).astype(BF16)
        qt = _bdot_nt(wqt_ref[h * QK_PACK:(h + 1) * QK_PACK, :], cqn)
        x1, x2 = qt[r0:r1], qt[r1:r2]
        qt = jnp.concatenate([qt[:r0], x1 * cos_t - x2 * sin_t, x2 * cos_t + x1 * sin_t, qt[r2:]], axis=0)
        qt_ref[0, h, 0] = (qt * scale).astype(BF16)


def _mla_up(cq, ckv, kr, posc, posr, gq, gkv, wqt, wk, wvt, invr, invc, tm):
    b, s, _ = cq.shape
    n = s // tm
    tok = lambda w: pl.BlockSpec((1, tm, w), lambda bi, i: (bi, i, 0))
    full = lambda a: pl.BlockSpec(a.shape, lambda bi, i: (0,) * a.ndim)
    scale = float(QK_NOPE + QK_ROPE) ** -0.5 * math.log2(math.e)
    return pl.pallas_call(
        functools.partial(_mla_up_kernel, scale=scale),
        grid=(b, n),
        in_specs=[tok(Q_LORA), tok(KV_LORA), tok(LANES), tok(1),
                  pl.BlockSpec((1, 1, tm), lambda bi, i: (bi, 0, i)),
                  full(gq), full(gkv), full(wqt), full(wk), full(wvt), full(invr), full(invc)],
        out_specs=[pl.BlockSpec((1, MLA_HEADS, 1, QK_PACK, tm), lambda bi, i: (bi, 0, i, 0, 0)),
                   pl.BlockSpec((1, MLA_HEADS, tm, QK_PACK), lambda bi, i: (bi, 0, i, 0)),
                   pl.BlockSpec((1, MLA_HEADS, 1, V_HEAD, tm), lambda bi, i: (bi, 0, i, 0, 0))],
        out_shape=[jax.ShapeDtypeStruct((b, MLA_HEADS, n, QK_PACK, tm), BF16),
                   jax.ShapeDtypeStruct((b, MLA_HEADS, s, QK_PACK), BF16),
                   jax.ShapeDtypeStruct((b, MLA_HEADS, n, V_HEAD, tm), BF16)],
        compiler_params=_cparams(("parallel", "parallel")),
        name="mla_up",
    )(cq, ckv, kr, posc, posr, gq, gkv, wqt, wk, wvt, invr, invc)


def _flash_kernel(qt_ref, k_ref, vt_ref, o_ref, *, tq, groups):
    i = pl.program_id(2)
    gw = tq // groups
    qts = [qt_ref[0, 0, 0, :, g * gw:(g + 1) * gw] for g in range(groups)]

    def scores(j):
        kj = k_ref[0, 0, pl.ds(pl.multiple_of(j * tq, tq), tq), :]
        return [jnp.dot(kj, qts[g], preferred_element_type=F32) for g in range(groups)]

    def softmax(st, m, l, g, masked):
        if masked:
            key = lax.broadcasted_iota(jnp.int32, st.shape, 0)
            qry = g * gw + lax.broadcasted_iota(jnp.int32, st.shape, 1)
            st = jnp.where(key <= qry, st, -jnp.inf)
        m_new = jnp.maximum(m, jnp.max(st, axis=0, keepdims=True))
        alpha = jnp.exp2(m - m_new)
        p = jnp.exp2(st - m_new)
        return m_new, alpha * l + jnp.sum(p, axis=0, keepdims=True), alpha, p.astype(BF16)

    def accumulate(acc, alpha, j, p):
        return alpha * acc + jnp.dot(vt_ref[0, 0, j], p, preferred_element_type=F32)

    def body(t, carry):
        st, m, l, alpha, p, acc = carry
        st_next = scores(t + 1)
        out = [softmax(st[g], m[g], l[g], g, False) for g in range(groups)]
        acc = [accumulate(acc[g], alpha[g], jnp.maximum(t - 1, 0), p[g]) for g in range(groups)]
        m, l, alpha, p = zip(*out)
        return st_next, list(m), list(l), list(alpha), list(p), acc

    per_group = lambda f: [f() for _ in range(groups)]
    carry = (scores(0),
             per_group(lambda: jnp.full((1, gw), -jnp.inf, F32)), per_group(lambda: jnp.zeros((1, gw), F32)),
             per_group(lambda: jnp.ones((1, gw), F32)), per_group(lambda: jnp.zeros((tq, gw), BF16)),
             per_group(lambda: jnp.zeros((V_HEAD, gw), F32)))
    st, m, l, alpha, p, acc = lax.fori_loop(0, i, body, carry)
    for g in range(groups):
        _, l_fin, alpha_fin, p_fin = softmax(st[g], m[g], l[g], g, True)
        a = accumulate(acc[g], alpha[g], jnp.maximum(i - 1, 0), p[g])
        a = accumulate(a, alpha_fin, i, p_fin)
        o_ref[0, g * gw:(g + 1) * gw, :] = (a / l_fin).T.astype(o_ref.dtype)


def _flash(qt, k, vt, groups):
    b, _, n, _, tq = qt.shape
    s = n * tq
    return pl.pallas_call(
        functools.partial(_flash_kernel, tq=tq, groups=groups),
        grid=(b, MLA_HEADS, n),
        in_specs=[pl.BlockSpec((1, 1, 1, QK_PACK, tq), lambda bi, h, i: (bi, h, i, 0, 0)),
                  pl.BlockSpec((1, 1, s, QK_PACK), lambda bi, h, i: (bi, h, 0, 0)),
                  pl.BlockSpec((1, 1, n, V_HEAD, tq), lambda bi, h, i: (bi, h, 0, 0, 0))],
        out_specs=pl.BlockSpec((1, tq, V_HEAD), lambda bi, h, i: (bi, i, h)),
        out_shape=jax.ShapeDtypeStruct((b, s, MIX_WIDTH), BF16),
        compiler_params=_cparams(("parallel", "parallel", "arbitrary")),
        name="flash",
    )(qt, k, vt)


def _seg_sum(x, segm):
    return _split_dot(x, segm, 2)


def _rwkv_prep_kernel(ur_ref, uk_ref, uv_ref, uwa_ref, pr_ref, pk_ref, pv_ref, pwa_ref,
                      mur_ref, muk_ref, muv_ref, muwa_ref, w0_ref, a0_ref, wl_ref, kkw_ref, kaw_ref, rkw_ref,
                      lmat_ref, segm_ref,
                      r_o, lw_o, g_o, k_o, v_o, kk_o, bb_o, bonus_o, *, prev_rows):
    first = pl.program_id(1) == 0

    def mix(cur_ref, prev_ref, mu_ref):
        u = cur_ref[0].astype(F32)
        last = prev_ref[0][prev_rows - 1:prev_rows, :].astype(F32)
        last = jnp.where(first, 0.0, last)
        rid = lax.broadcasted_iota(jnp.int32, u.shape, 0)
        up = jnp.where(rid == 0, last, pltpu.roll(u, 1, 0))
        return u + (up - u) * mu_ref[...]

    xr = mix(ur_ref, pr_ref, mur_ref)
    xk = mix(uk_ref, pk_ref, muk_ref)
    xv = mix(uv_ref, pv_ref, muv_ref)
    xwa = mix(uwa_ref, pwa_ref, muwa_ref)

    lane = lax.broadcasted_iota(jnp.int32, xwa.shape, 1)
    z = jnp.where(lane < DECAY_LORA, jnp.tanh(xwa), xwa)
    lo = jnp.dot(z.astype(BF16), wl_ref[...], preferred_element_type=F32)
    zw = -(w0_ref[...] + lo[:, :MIX_WIDTH])
    softplus = jnp.maximum(zw, 0.0) + jnp.log1p(jnp.exp(-jnp.abs(zw)))
    lw = -jnp.exp(-softplus - 0.5)
    a = 1.0 / (1.0 + jnp.exp(-(a0_ref[...] + lo[:, MIX_WIDTH:])))
    kkr = xk * kkw_ref[...]
    kmod = xk * (1.0 + (a - 1.0) * kaw_ref[...])
    rkk = xr * kmod * rkw_ref[...]
    lmat = lmat_ref[...]
    segm = segm_ref[...]
    for p in range(N_PAIRS):
        sl = slice(p * LANES, (p + 1) * LANES)
        n2 = _seg_sum(kkr[:, sl] * kkr[:, sl], segm)
        kk = kkr[:, sl] / jnp.maximum(jnp.sqrt(n2), 1e-12)
        r_o[0, p] = xr[:, sl]
        lw_o[0, p] = lw[:, sl]
        g_o[0, p] = _split_dot_left(lmat, lw[:, sl])
        k_o[0, p] = kmod[:, sl]
        v_o[0, p] = xv[:, sl]
        kk_o[0, p] = kk
        bb_o[0, p] = kk * a[:, sl]
        bonus_o[0, p] = _seg_sum(rkk[:, sl], segm) * xv[:, sl]


def _split_dot_left(m_bf16, x):
    acc = None
    rem = x
    for _ in range(3):
        piece = rem.astype(BF16)
        part = jnp.dot(m_bf16, piece, preferred_element_type=F32)
        acc = part if acc is None else acc + part
        rem = rem - piece.astype(F32)
    return acc


def _rwkv_prep(ur, uk, uv, uwa, mur, muk, muv, muwa, w0, a0, wl, kkw, kaw, rkw, lmat, segm, ts):
    b, s, _ = ur.shape
    prev_rows = 16
    cur = lambda w: pl.BlockSpec((1, ts, w), lambda bi, i: (bi, i, 0))
    prev = lambda w: pl.BlockSpec((1, prev_rows, w),
                                  lambda bi, i: (bi, jnp.maximum(i * (ts // prev_rows) - 1, 0), 0))
    full = lambda a: pl.BlockSpec(a.shape, lambda bi, i: (0,) * a.ndim)
    pm = pl.BlockSpec((1, N_PAIRS, ts, LANES), lambda bi, i: (bi, 0, i, 0))
    small = [mur, muk, muv, muwa, w0, a0, wl, kkw, kaw, rkw, lmat, segm]
    return pl.pallas_call(
        functools.partial(_rwkv_prep_kernel, prev_rows=prev_rows),
        grid=(b, s // ts),
        in_specs=[cur(MIX_WIDTH), cur(MIX_WIDTH), cur(MIX_WIDTH), cur(LANES),
                  prev(MIX_WIDTH), prev(MIX_WIDTH), prev(MIX_WIDTH), prev(LANES)] + [full(a) for a in small],
        out_specs=[pm] * 8,
        out_shape=[jax.ShapeDtypeStruct((b, N_PAIRS, s, LANES), F32)] * 8,
        compiler_params=_cparams(("parallel", "parallel")),
        name="rwkv_prep",
    )(ur, uk, uv, uwa, ur, uk, uv, uwa, *small)


def _rwkv_scan_kernel(r_ref, lw_ref, g_ref, k_ref, v_ref, kk_ref, bb_ref, bonus_ref, gnw_ref, gnb_ref, segm_ref,
                      o_ref, s_ref):
    c2 = 2 * CHUNK

    @pl.when(pl.program_id(1) == 0)
    def _():
        s_ref[...] = jnp.zeros_like(s_ref)

    lane = lax.broadcasted_iota(jnp.int32, (CHUNK, LANES), 1)
    head0 = lane < RWKV_HEAD

    def stack(x):
        return jnp.concatenate([jnp.where(head0, x, 0.0), jnp.where(head0, 0.0, x)], axis=0)

    row = lax.broadcasted_iota(jnp.int32, (2 * c2, 2 * c2), 0)
    col = lax.broadcasted_iota(jnp.int32, (2 * c2, 2 * c2), 1)
    same_head = ((row // CHUNK) % 2) == ((col // CHUNK) % 2)
    tr, tc = row % CHUNK, col % CHUNK
    aa_mask = same_head & ((tc < tr) | ((row >= c2) & (tc == tr)))
    segm = segm_ref[...]

    pairs = range(N_PAIRS)
    g = [g_ref[0, p] for p in pairs]
    g_last = [gp[CHUNK - 1:CHUNK, :] for gp in g]
    ats = [stack(-kk_ref[0, p] * jnp.exp(g[p] - lw_ref[0, p])) for p in pairs]
    rts = [stack(r_ref[0, p] * jnp.exp(g[p])) for p in pairs]
    vs = [stack(v_ref[0, p]) for p in pairs]
    aa = []
    for p in pairs:
        e_neg = jnp.exp(-g[p])
        bt, kt = bb_ref[0, p] * e_neg, k_ref[0, p] * e_neg
        a = _bdot_nt(jnp.concatenate([ats[p], rts[p]], axis=0), jnp.concatenate([bt, bt, kt, kt], axis=0))
        aa.append(jnp.where(aa_mask, a, 0.0))
    x = [a[:c2, :c2] for a in aa]
    z = [jnp.concatenate([ats[p], _bdot(aa[p][:c2, c2:], vs[p])], axis=1) for p in pairs]
    steps = int(math.log2(CHUNK))
    for lvl in range(steps):
        z = [z[p] + _bdot(x[p], z[p]) for p in pairs]
        if lvl + 1 < steps:
            x = [_bdot(x[p], x[p]) for p in pairs]
    s0 = [s_ref[p] for p in pairs]
    us = [_bdot_nt(z[p][:, :LANES], s0[p]) + z[p][:, LANES:] for p in pairs]
    ys = [_bdot_nt(rts[p], s0[p]) + _bdot(aa[p][c2:, :c2], us[p]) + _bdot(aa[p][c2:, c2:], vs[p]) for p in pairs]
    for p in pairs:
        e_rel = jnp.exp(g_last[p] - g[p])
        s_ref[p] = s0[p] * jnp.exp(g_last[p]) + _bdot_tn(
            jnp.concatenate([us[p], vs[p]], axis=0),
            jnp.concatenate([stack(bb_ref[0, p] * e_rel), stack(k_ref[0, p] * e_rel)], axis=0))
    for p in pairs:
        y = ys[p][:CHUNK] + ys[p][CHUNK:]
        mean = _seg_sum(y, segm) * (1.0 / RWKV_HEAD)
        d = y - mean
        var = _seg_sum(d * d, segm) * (1.0 / RWKV_HEAD)
        out = d * lax.rsqrt(var + GN_EPS) * gnw_ref[p] + gnb_ref[p] + bonus_ref[0, p]
        o_ref[0, :, p * LANES:(p + 1) * LANES] = out.astype(o_ref.dtype)


def _rwkv_scan(r, lw, g, k, v, kk, bb, bonus, gnw, gnb, segm):
    b, _, s, _ = r.shape
    pm = pl.BlockSpec((1, N_PAIRS, CHUNK, LANES), lambda bi, c: (bi, 0, c, 0))
    full = lambda a: pl.BlockSpec(a.shape, lambda bi, c: (0,) * a.ndim)
    return pl.pallas_call(
        _rwkv_scan_kernel,
        grid=(b, s // CHUNK),
        in_specs=[pm] * 8 + [full(gnw), full(gnb), full(segm)],
        out_specs=pl.BlockSpec((1, CHUNK, MIX_WIDTH), lambda bi, c: (bi, c, 0)),
        out_shape=jax.ShapeDtypeStruct((b, s, MIX_WIDTH), BF16),
        scratch_shapes=[pltpu.VMEM((N_PAIRS, LANES, LANES), F32)],
        compiler_params=_cparams(("parallel", "arbitrary")),
        name="rwkv_scan",
    )(r, lw, g, k, v, kk, bb, bonus, gnw, gnb, segm)


def _out_kernel(x_ref, mix_ref, qm_ref, gate_ref, mk_ref, mv_ref, wo_ref, o_ref):
    gate = gate_ref[0].astype(F32)
    sg = gate * (1.0 / (1.0 + jnp.exp(-gate)))
    acc = x_ref[0] + _bdot(mix_ref[0].astype(F32) * sg[:, :MIX_WIDTH], wo_ref[:MIX_WIDTH, :])
    heads = []
    for h in range(X_HEADS):
        sl = slice(h * X_HEAD_DIM, (h + 1) * X_HEAD_DIM)
        s = lax.dot_general(qm_ref[0][:, sl], mk_ref[0][:, sl], (((1,), (1,)), ((), ())),
                            preferred_element_type=F32) * (X_HEAD_DIM ** -0.5)
        p = jnp.exp(s - jnp.max(s, axis=-1, keepdims=True))
        p = p / jnp.sum(p, axis=-1, keepdims=True)
        heads.append(jnp.dot(p.astype(BF16), mv_ref[0][:, sl], preferred_element_type=F32))
    mem_out = jnp.concatenate(heads, axis=1)
    acc = acc + _bdot(mem_out * sg[:, MIX_WIDTH:], wo_ref[MIX_WIDTH:, :])
    o_ref[0] = acc


def _out(x, mix, qm, gate, mk, mv, wo, ts):
    b, s, d = x.shape
    n_mem = mk.shape[1]
    tok = lambda w: pl.BlockSpec((1, ts, w), lambda bi, i: (bi, i, 0))
    memspec = pl.BlockSpec((1, n_mem, X_WIDTH), lambda bi, i: (bi, 0, 0))
    return pl.pallas_call(
        _out_kernel,
        grid=(b, s // ts),
        in_specs=[tok(d), tok(MIX_WIDTH), tok(X_WIDTH), tok(INNER), memspec, memspec,
                  pl.BlockSpec(wo.shape, lambda bi, i: (0, 0))],
        out_specs=tok(d),
        out_shape=jax.ShapeDtypeStruct((b, s, d), F32),
        compiler_params=_cparams(("parallel", "parallel")),
        name="out_proj",
    )(x, mix, qm, gate, mk, mv, wo)


def _final_norm_kernel(x_ref, g_ref, o_ref):
    x = x_ref[...]
    ms = jnp.mean(x * x, axis=-1, keepdims=True)
    o_ref[...] = x * lax.rsqrt(ms + NORM_EPS) * g_ref[...]


def _final_norm(x2d, g, tm):
    m, d = x2d.shape
    return pl.pallas_call(
        _final_norm_kernel,
        grid=(m // tm,),
        in_specs=[pl.BlockSpec((tm, d), lambda i: (i, 0)), pl.BlockSpec((1, d), lambda i: (0, 0))],
        out_specs=pl.BlockSpec((tm, d), lambda i: (i, 0)),
        out_shape=jax.ShapeDtypeStruct((m, d), F32),
        compiler_params=_cparams(("parallel",)),
        name="final_norm",
    )(x2d, g.reshape(1, d))


def _rope_perm():
    return np.concatenate([np.arange(0, QK_ROPE, 2), np.arange(1, QK_ROPE, 2)])


def _pack_mla_weights(w_in, w_uq, w_ukv):
    cq_w, ckv_w, kr_w, qm_w, gate_w = jnp.split(
        w_in, np.cumsum([Q_LORA, KV_LORA, QK_ROPE, X_WIDTH]).tolist(), axis=1)
    perm = _rope_perm()
    kr_w = jnp.pad(kr_w[:, perm], ((0, 0), (0, LANES - QK_ROPE)))
    wq = w_uq.reshape(Q_LORA, MLA_HEADS, QK_NOPE + QK_ROPE)
    wq = jnp.concatenate([wq[:, :, :QK_NOPE], wq[:, :, QK_NOPE:][:, :, perm],
                          jnp.zeros((Q_LORA, MLA_HEADS, QK_PACK - QK_NOPE - QK_ROPE), w_uq.dtype)], axis=2)
    wqt = wq.reshape(Q_LORA, MLA_HEADS * QK_PACK).T
    wkv = w_ukv.reshape(KV_LORA, MLA_HEADS, QK_NOPE + V_HEAD)
    wk = wkv[:, :, :QK_NOPE].reshape(KV_LORA, MLA_HEADS * QK_NOPE)
    wvt = wkv[:, :, QK_NOPE:].reshape(KV_LORA, MLA_HEADS * V_HEAD).T
    bf = lambda a: a.astype(BF16)
    return [bf(cq_w), bf(ckv_w), bf(kr_w), bf(qm_w), bf(gate_w)], bf(wqt), bf(wk), bf(wvt)


def _rope_freqs():
    inv_freq = ROPE_THETA ** (-jnp.arange(0, QK_ROPE, 2, dtype=F32) / QK_ROPE)
    lanes = jnp.concatenate([inv_freq, inv_freq, jnp.zeros((LANES - QK_ROPE,), F32)]).reshape(1, LANES)
    return lanes, inv_freq.reshape(HALF_ROPE, 1)


def _chunk_tri(ts):
    t = np.arange(ts)
    return jnp.asarray((t[:, None] // CHUNK == t[None, :] // CHUNK) & (t[None, :] <= t[:, None]), BF16)


def _head_seg():
    t = np.arange(LANES)
    return jnp.asarray(t[:, None] // RWKV_HEAD == t[None, :] // RWKV_HEAD, BF16)


def kernel(x, mem, positions, norm_g, mem_norm_g, w_mem_kv, w_in_mla, mla_q_norm_g, mla_kv_norm_g, mla_w_uq,
           mla_w_ukv, w_in_rwkv, rwkv_mu, rwkv_w0, rwkv_w2, rwkv_a0, rwkv_a2, rwkv_k_k, rwkv_k_a, rwkv_r_k,
           rwkv_gn_w, rwkv_gn_b, w_out, final_g):
    b, s, d = x.shape
    n_mem = mem.shape[1]
    depth = norm_g.shape[0]
    t = b * s
    tm = min(512, s)
    posc = positions.reshape(b, s, 1).astype(F32)
    posr = positions.reshape(b, 1, s).astype(F32)
    invr, invc = _rope_freqs()
    segm = _head_seg()
    prep_ts = min(128, s)
    lmat = _chunk_tri(prep_ts)
    row = lambda a: a.reshape(1, -1).astype(F32)

    for i in range(depth):
        j = i // 2
        mk, mv = _norm_proj(mem.reshape(b * n_mem, d), mem_norm_g[i],
                            [w_mem_kv[i][:, :X_WIDTH].astype(BF16), w_mem_kv[i][:, X_WIDTH:].astype(BF16)],
                            [BF16, BF16], tm=min(256, b * n_mem))
        mk = mk.reshape(b, n_mem, X_WIDTH)
        mv = mv.reshape(b, n_mem, X_WIDTH)
        if i % 2 == 0:
            w_list, wqt, wk, wvt = _pack_mla_weights(w_in_mla[j], mla_w_uq[j], mla_w_ukv[j])
            cq, ckv, kr, qm, gate = _norm_proj(x.reshape(t, d), norm_g[i], w_list, [F32, F32, F32, BF16, BF16],
                                               tm=tm)
            sh = lambda a: a.reshape(b, s, -1)
            qt, k, vt = _mla_up(sh(cq), sh(ckv), sh(kr), posc, posr, row(mla_q_norm_g[j]), row(mla_kv_norm_g[j]),
                                wqt, wk, wvt, invr, invc, tm=min(FLASH_TILE, s))
            mix = _flash(qt, k, vt, groups=2)
        else:
            w = w_in_rwkv[j]
            edges = np.cumsum([MIX_WIDTH, MIX_WIDTH, MIX_WIDTH, DECAY_LORA + ICLR_LORA, X_WIDTH]).tolist()
            w_list = [a.astype(BF16) for a in jnp.split(w, edges, axis=1)]
            ur, uk, uv, uwa, qm, gate = _norm_proj(x.reshape(t, d), norm_g[i], w_list, [BF16] * 6, tm=min(256, s))
            mu = rwkv_mu[j]
            mur, muk, muv, muwa = [row(a) for a in jnp.split(mu, edges[:3])]
            zeros = jnp.zeros((DECAY_LORA, MIX_WIDTH), F32)
            wl = jnp.concatenate([jnp.concatenate([rwkv_w2[j], zeros], axis=1),
                                  jnp.concatenate([zeros, rwkv_a2[j]], axis=1)], axis=0).astype(BF16)
            sh = lambda a: a.reshape(b, s, -1)
            r, lw, g, kmod, vv, kk, bb, bonus = _rwkv_prep(
                sh(ur), sh(uk), sh(uv), sh(uwa), mur, muk, muv, muwa, row(rwkv_w0[j]), row(rwkv_a0[j]), wl,
                row(rwkv_k_k[j]), row(rwkv_k_a[j]), row(rwkv_r_k[j]), lmat, segm, ts=prep_ts)
            gnw = rwkv_gn_w[j].reshape(N_PAIRS, 1, LANES)
            gnb = rwkv_gn_b[j].reshape(N_PAIRS, 1, LANES)
            mix = _rwkv_scan(r, lw, g, kmod, vv, kk, bb, bonus, gnw, gnb, segm)
        x = _out(x, mix, qm.reshape(b, s, -1), gate.reshape(b, s, -1), mk, mv, w_out[i].astype(BF16),
                 ts=min(512, s))
    return _final_norm(x.reshape(t, d), final_g, tm=tm).reshape(b, s, d)
```

```python
import functools
import math

import jax
import jax.numpy as jnp
import numpy as np
from jax import lax
from jax.experimental import pallas as pl
from jax.experimental.pallas import tpu as pltpu

F32 = jnp.float32
BF16 = jnp.bfloat16

D_MODEL = 1024
X_HEADS = 4
X_HEAD_DIM = 128
X_WIDTH = X_HEADS * X_HEAD_DIM
MLA_HEADS = 12
QK_NOPE = 128
QK_ROPE = 64
V_HEAD = 128
Q_LORA = 384
KV_LORA = 256
ROPE_THETA = 10000.0
MIX_WIDTH = MLA_HEADS * V_HEAD
INNER = MIX_WIDTH + X_WIDTH
RWKV_HEAD = 64
RWKV_HEADS = MIX_WIDTH // RWKV_HEAD
DECAY_LORA = 64
ICLR_LORA = 64
GN_EPS = 64e-5
NORM_EPS = 1e-6

LANES = 128
QK_PACK = 2 * LANES
N_PAIRS = MIX_WIDTH // LANES
CHUNK = 64
FLASH_TILE = 1024
FLASH_GROUPS = 4
MLA_UP_TILE = 512
BF16_SUBLANES = 16
V_AUG = V_HEAD + BF16_SUBLANES
VMEM_LIMIT = 56 * 1024 * 1024


def _cparams(sem):
    return pltpu.CompilerParams(dimension_semantics=sem, vmem_limit_bytes=VMEM_LIMIT)


def _bdot(a, b):
    return jnp.dot(a.astype(BF16), b.astype(BF16), preferred_element_type=F32)


def _bdot_nt(a, b):
    return lax.dot_general(a.astype(BF16), b.astype(BF16), (((1,), (1,)), ((), ())),
                           preferred_element_type=F32)


def _bdot_tn(a, b):
    return lax.dot_general(a.astype(BF16), b.astype(BF16), (((0,), (0,)), ((), ())),
                           preferred_element_type=F32)


def _split_dot(x, m_bf16, terms):
    acc = None
    rem = x
    for _ in range(terms):
        piece = rem.astype(BF16)
        part = jnp.dot(piece, m_bf16, preferred_element_type=F32)
        acc = part if acc is None else acc + part
        rem = rem - piece.astype(F32)
    return acc


def _norm_proj_kernel(x_ref, g_ref, *refs, n_out):
    w_refs, o_refs = refs[:n_out], refs[n_out:]
    x = x_ref[...].astype(F32)
    ms = jnp.mean(x * x, axis=-1, keepdims=True)
    xn = (x * lax.rsqrt(ms + NORM_EPS) * g_ref[...]).astype(BF16)
    for w_ref, o_ref in zip(w_refs, o_refs):
        o_ref[...] = jnp.dot(xn, w_ref[...], preferred_element_type=F32).astype(o_ref.dtype)


def _norm_proj(x2d, g, weights, out_dtypes, tm):
    m, k = x2d.shape
    n_out = len(weights)
    in_specs = [pl.BlockSpec((tm, k), lambda i: (i, 0)), pl.BlockSpec((1, k), lambda i: (0, 0))]
    in_specs += [pl.BlockSpec(w.shape, lambda i: (0, 0)) for w in weights]
    out_specs = [pl.BlockSpec((tm, w.shape[1]), lambda i: (i, 0)) for w in weights]
    out_shape = [jax.ShapeDtypeStruct((m, w.shape[1]), dt) for w, dt in zip(weights, out_dtypes)]
    return pl.pallas_call(
        functools.partial(_norm_proj_kernel, n_out=n_out),
        grid=(m // tm,),
        in_specs=in_specs,
        out_specs=out_specs,
        out_shape=out_shape,
        compiler_params=_cparams(("parallel",)),
        name="norm_proj",
    )(x2d, g.reshape(1, k).astype(F32), *weights)


HALF_ROPE = QK_ROPE // 2


def _mla_up_kernel(cq_ref, ckv_ref, kr_ref, posc_ref, posr_ref, gq_ref, gkv_ref, wqt_ref, wk_ref, wvt_ref,
                   invr_ref, invc_ref, qt_ref, k_ref, vt_ref, *, scale):
    def rms(x, g):
        ms = jnp.mean(x * x, axis=-1, keepdims=True)
        return (x * lax.rsqrt(ms + NORM_EPS) * g).astype(BF16)

    cqn = rms(cq_ref[0], gq_ref[...])
    ckvn = rms(ckv_ref[0], gkv_ref[...])

    ang = posc_ref[0] * invr_ref[...]
    lane = lax.broadcasted_iota(jnp.int32, ang.shape, 1)
    cosv, sinv = jnp.cos(ang), jnp.sin(ang)
    kr = kr_ref[0]
    kr = (kr * cosv
          + pltpu.roll(kr, LANES - HALF_ROPE, 1) * jnp.where(lane < HALF_ROPE, -sinv, 0.0)
          + pltpu.roll(kr, HALF_ROPE, 1) * jnp.where((lane >= HALF_ROPE) & (lane < QK_ROPE), sinv, 0.0))
    kr = kr.astype(BF16)
    kn = jnp.dot(ckvn, wk_ref[...], preferred_element_type=F32).astype(BF16)

    ang_t = invc_ref[...] * posr_ref[0]
    cos_t, sin_t = jnp.cos(ang_t), jnp.sin(ang_t)
    r0, r1, r2 = QK_NOPE, QK_NOPE + HALF_ROPE, QK_NOPE + QK_ROPE
    ones_row = (lax.broadcasted_iota(jnp.int32, (V_AUG - V_HEAD, ang_t.shape[1]), 0) == 0).astype(BF16)
    for h in range(MLA_HEADS):
        k_ref[0, h, :, :LANES] = kn[:, h * LANES:(h + 1) * LANES]
        k_ref[0, h, :, LANES:] = kr
        vt_ref[0, h, 0, :V_HEAD] = _bdot_nt(wvt_ref[h * V_HEAD:(h + 1) * V_HEAD, :], ckvn).astype(BF16)
        vt_ref[0, h, 0, V_HEAD:] = ones_row
        qt = _bdot_nt(wqt_ref[h * QK_PACK:(h + 1) * QK_PACK, :], cqn)
        x1, x2 = qt[r0:r1], qt[r1:r2]
        qt = jnp.concatenate([qt[:r0], x1 * cos_t - x2 * sin_t, x2 * cos_t + x1 * sin_t, qt[r2:]], axis=0)
        qt_ref[0, h, 0] = (qt * scale).astype(BF16)


def _mla_up(cq, ckv, kr, posc, posr, gq, gkv, wqt, wk, wvt, invr, invc, tm, tq):
    b, s, _ = cq.shape
    n, r = s // tq, tq // tm
    tok = lambda w: pl.BlockSpec((1, tm, w), lambda bi, i: (bi, i, 0))
    full = lambda a: pl.BlockSpec(a.shape, lambda bi, i: (0,) * a.ndim)
    scale = float(QK_NOPE + QK_ROPE) ** -0.5 * math.log2(math.e)
    return pl.pallas_call(
        functools.partial(_mla_up_kernel, scale=scale),
        grid=(b, s // tm),
        in_specs=[tok(Q_LORA), tok(KV_LORA), tok(LANES), tok(1),
                  pl.BlockSpec((1, 1, tm), lambda bi, i: (bi, 0, i)),
                  full(gq), full(gkv), full(wqt), full(wk), full(wvt), full(invr), full(invc)],
        out_specs=[pl.BlockSpec((1, MLA_HEADS, 1, QK_PACK, tm), lambda bi, i: (bi, 0, i // r, 0, i % r)),
                   pl.BlockSpec((1, MLA_HEADS, tm, QK_PACK), lambda bi, i: (bi, 0, i, 0)),
                   pl.BlockSpec((1, MLA_HEADS, 1, V_AUG, tm), lambda bi, i: (bi, 0, i // r, 0, i % r))],
        out_shape=[jax.ShapeDtypeStruct((b, MLA_HEADS, n, QK_PACK, tq), BF16),
                   jax.ShapeDtypeStruct((b, MLA_HEADS, s, QK_PACK), BF16),
                   jax.ShapeDtypeStruct((b, MLA_HEADS, n, V_AUG, tq), BF16)],
        compiler_params=_cparams(("parallel", "parallel")),
        name="mla_up",
    )(cq, ckv, kr, posc, posr, gq, gkv, wqt, wk, wvt, invr, invc)


def _flash_kernel(qt_ref, k_ref, vt_ref, o_ref, st_ref, p_ref, acc_ref, *, tq, groups):
    i = pl.program_id(2)
    gw = tq // groups
    every = tuple(range(groups))
    qts = [qt_ref[0, 0, 0, :, g * gw:(g + 1) * gw] for g in every]

    def scores(tile, d, which):
        mb = {}
        if which:
            kj = k_ref[0, 0, pl.ds(pl.multiple_of(tile * tq, tq) + d * gw, gw), :]
        for g in which:
            st = jnp.dot(kj, qts[g], preferred_element_type=F32)
            st_ref[d % 2, g] = st
            mb[g] = jnp.max(st, axis=0, keepdims=True)
        return mb

    def softmax(d, g, mb, m, masked):
        st = st_ref[d % 2, g]
        if masked:
            key = lax.broadcasted_iota(jnp.int32, st.shape, 0)
            qry = lax.broadcasted_iota(jnp.int32, st.shape, 1)
            st = jnp.where(key <= qry, st, -jnp.inf)
            mb = jnp.max(st, axis=0, keepdims=True)
        m_new = jnp.maximum(m, mb)
        p_ref[d % 2, g] = jnp.exp2(st - m_new).astype(BF16)
        return m_new, jnp.exp2(m - m_new)

    def accumulate(tile, d, g, alpha):
        vtj = vt_ref[0, 0, tile, :, d * gw:(d + 1) * gw]
        acc_ref[g] = alpha * acc_ref[g] + jnp.dot(vtj, p_ref[d % 2, g], preferred_element_type=F32)

    def step(tile, d, state, b_groups, a_groups, c_groups, masked_group):
        mb, m, alpha = state
        nxt = (tile, d + 1) if d + 1 < groups else (tile + 1, 0)
        prv = (tile, d - 1) if d > 0 else (jnp.maximum(tile - 1, 0), groups - 1)
        mb_next = scores(*nxt, a_groups)
        for g in c_groups:
            accumulate(*prv, g, alpha[g])
        m, alpha = list(m), list(alpha)
        for g in b_groups:
            m[g], alpha[g] = softmax(d, g, mb[g], m[g], masked=(g == masked_group))
        return [mb_next.get(g, mb[g]) for g in every], m, alpha

    def trip(u, state):
        for d in every:
            state = step(u, d, state, every, every, every, None)
        return state

    p_ref[(groups - 1) % 2] = jnp.zeros(p_ref.shape[1:], BF16)
    acc_ref[...] = jnp.zeros_like(acc_ref)
    mb0 = scores(0, 0, every)
    per_group = lambda v: [jnp.full((1, gw), v, F32) for _ in every]
    state = ([mb0[g] for g in every], per_group(-jnp.inf), per_group(1.0))
    state = lax.fori_loop(0, i, trip, state)

    for d in every:
        state = step(i, d, state, every[d:], every[d + 1:], every if d == 0 else every[d - 1:], d)
    alpha = state[2]
    accumulate(i, groups - 1, groups - 1, alpha[groups - 1])
    for g in every:
        acc = acc_ref[g]
        out = acc[:V_HEAD] / acc[V_HEAD:V_HEAD + 1]
        o_ref[0, g * gw:(g + 1) * gw, :] = out.T.astype(o_ref.dtype)


def _flash(qt, k, vt, groups):
    b, _, n, _, tq = qt.shape
    s = n * tq
    gw = tq // groups
    return pl.pallas_call(
        functools.partial(_flash_kernel, tq=tq, groups=groups),
        grid=(b, MLA_HEADS, n),
        in_specs=[pl.BlockSpec((1, 1, 1, QK_PACK, tq), lambda bi, h, i: (bi, h, i, 0, 0)),
                  pl.BlockSpec((1, 1, s, QK_PACK), lambda bi, h, i: (bi, h, 0, 0)),
                  pl.BlockSpec((1, 1, n, V_AUG, tq), lambda bi, h, i: (bi, h, 0, 0, 0))],
        out_specs=pl.BlockSpec((1, tq, V_HEAD), lambda bi, h, i: (bi, i, h)),
        out_shape=jax.ShapeDtypeStruct((b, s, MIX_WIDTH), BF16),
        scratch_shapes=[pltpu.VMEM((2, groups, gw, gw), F32), pltpu.VMEM((2, groups, gw, gw), BF16),
                        pltpu.VMEM((groups, V_AUG, gw), F32)],
        compiler_params=_cparams(("parallel", "parallel", "arbitrary")),
        name="flash",
    )(qt, k, vt)


def _seg_sum(x, segm):
    return _split_dot(x, segm, 2)


def _rwkv_prep_kernel(ur_ref, uk_ref, uv_ref, uwa_ref, pr_ref, pk_ref, pv_ref, pwa_ref,
                      mur_ref, muk_ref, muv_ref, muwa_ref, w0_ref, a0_ref, wl_ref, kkw_ref, kaw_ref, rkw_ref,
                      lmat_ref, segm_ref,
                      r_o, g_o, k_o, v_o, kk_o, bb_o, bonus_o, *, prev_rows):
    first = pl.program_id(1) == 0

    def mix(cur_ref, prev_ref, mu_ref):
        u = cur_ref[0].astype(F32)
        last = prev_ref[0][prev_rows - 1:prev_rows, :].astype(F32)
        last = jnp.where(first, 0.0, last)
        rid = lax.broadcasted_iota(jnp.int32, u.shape, 0)
        up = jnp.where(rid == 0, last, pltpu.roll(u, 1, 0))
        return u + (up - u) * mu_ref[...]

    xr = mix(ur_ref, pr_ref, mur_ref)
    xk = mix(uk_ref, pk_ref, muk_ref)
    xv = mix(uv_ref, pv_ref, muv_ref)
    xwa = mix(uwa_ref, pwa_ref, muwa_ref)

    lane = lax.broadcasted_iota(jnp.int32, xwa.shape, 1)
    z = jnp.where(lane < DECAY_LORA, jnp.tanh(xwa), xwa)
    lo = jnp.dot(z.astype(BF16), wl_ref[...], preferred_element_type=F32)
    lw = -math.exp(-0.5) / (1.0 + jnp.exp(-(w0_ref[...] + lo[:, :MIX_WIDTH])))
    a = 1.0 / (1.0 + jnp.exp(-(a0_ref[...] + lo[:, MIX_WIDTH:])))
    kkr = xk * kkw_ref[...]
    kmod = xk * (1.0 + (a - 1.0) * kaw_ref[...])
    rkk = xr * kmod * rkw_ref[...]
    lmat = lmat_ref[...]
    segm = segm_ref[...]
    for p in range(N_PAIRS):
        sl = slice(p * LANES, (p + 1) * LANES)
        n2 = _seg_sum(kkr[:, sl] * kkr[:, sl], segm)
        kk = kkr[:, sl] * lax.rsqrt(jnp.maximum(n2, 1e-24))
        r_o[0, p] = xr[:, sl].astype(r_o.dtype)
        g_o[0, p] = _split_dot_left(lmat, lw[:, sl])
        k_o[0, p] = kmod[:, sl].astype(k_o.dtype)
        v_o[0, p] = xv[:, sl].astype(v_o.dtype)
        kk_o[0, p] = kk.astype(kk_o.dtype)
        bb_o[0, p] = (kk * a[:, sl]).astype(bb_o.dtype)
        bonus_o[0, p] = (_seg_sum(rkk[:, sl], segm) * xv[:, sl]).astype(bonus_o.dtype)


def _split_dot_left(m_bf16, x):
    acc = None
    rem = x
    for _ in range(3):
        piece = rem.astype(BF16)
        part = jnp.dot(m_bf16, piece, preferred_element_type=F32)
        acc = part if acc is None else acc + part
        rem = rem - piece.astype(F32)
    return acc


def _rwkv_prep(ur, uk, uv, uwa, mur, muk, muv, muwa, w0, a0, wl, kkw, kaw, rkw, lmat, segm, ts):
    b, s, _ = ur.shape
    prev_rows = 16
    cur = lambda w: pl.BlockSpec((1, ts, w), lambda bi, i: (bi, i, 0))
    prev = lambda w: pl.BlockSpec((1, prev_rows, w),
                                  lambda bi, i: (bi, jnp.maximum(i * (ts // prev_rows) - 1, 0), 0))
    full = lambda a: pl.BlockSpec(a.shape, lambda bi, i: (0,) * a.ndim)
    pm = pl.BlockSpec((1, N_PAIRS, ts, LANES), lambda bi, i: (bi, 0, i, 0))
    small = [mur, muk, muv, muwa, w0, a0, wl, kkw, kaw, rkw, lmat, segm]
    return pl.pallas_call(
        functools.partial(_rwkv_prep_kernel, prev_rows=prev_rows),
        grid=(b, s // ts),
        in_specs=[cur(MIX_WIDTH), cur(MIX_WIDTH), cur(MIX_WIDTH), cur(LANES),
                  prev(MIX_WIDTH), prev(MIX_WIDTH), prev(MIX_WIDTH), prev(LANES)] + [full(a) for a in small],
        out_specs=[pm] * 7,
        out_shape=[jax.ShapeDtypeStruct((b, N_PAIRS, s, LANES), F32 if n == 1 else BF16) for n in range(7)],
        compiler_params=_cparams(("parallel", "parallel")),
        name="rwkv_prep",
    )(ur, uk, uv, uwa, ur, uk, uv, uwa, *small)


def _rwkv_scan_kernel(r_ref, g_ref, k_ref, v_ref, kk_ref, bb_ref, bonus_ref, gnw_ref, gnb_ref, segm_ref,
                      o_ref, s_ref):
    c2 = 2 * CHUNK

    @pl.when(pl.program_id(1) == 0)
    def _():
        s_ref[...] = jnp.zeros_like(s_ref)

    lane = lax.broadcasted_iota(jnp.int32, (CHUNK, LANES), 1)
    head0 = lane < RWKV_HEAD

    def stack(x):
        return jnp.concatenate([jnp.where(head0, x, 0.0), jnp.where(head0, 0.0, x)], axis=0)

    row = lax.broadcasted_iota(jnp.int32, (2 * c2, 2 * c2), 0)
    col = lax.broadcasted_iota(jnp.int32, (2 * c2, 2 * c2), 1)
    same_head = ((row // CHUNK) % 2) == ((col // CHUNK) % 2)
    tr, tc = row % CHUNK, col % CHUNK
    aa_mask = same_head & ((tc < tr) | ((row >= c2) & (tc == tr)))
    segm = segm_ref[...]

    pairs = range(N_PAIRS)
    g = [g_ref[0, p] for p in pairs]
    g_last = [gp[CHUNK - 1:CHUNK, :] for gp in g]
    first_row = lax.broadcasted_iota(jnp.int32, (CHUNK, LANES), 0) == 0
    g_prev = [jnp.where(first_row, 0.0, pltpu.roll(gp, 1, 0)) for gp in g]
    ats = [stack(-kk_ref[0, p] * jnp.exp(g_prev[p])) for p in pairs]
    rts = [stack(r_ref[0, p] * jnp.exp(g[p])) for p in pairs]
    vs = [stack(v_ref[0, p]) for p in pairs]
    aa = []
    for p in pairs:
        e_neg = jnp.exp(-g[p])
        bt, kt = bb_ref[0, p] * e_neg, k_ref[0, p] * e_neg
        a = _bdot_nt(jnp.concatenate([ats[p], rts[p]], axis=0), jnp.concatenate([bt, bt, kt, kt], axis=0))
        aa.append(jnp.where(aa_mask, a, 0.0))
    x = [a[:c2, :c2] for a in aa]
    z = [jnp.concatenate([ats[p], _bdot(aa[p][:c2, c2:], vs[p])], axis=1) for p in pairs]
    steps = int(math.log2(CHUNK))
    for lvl in range(steps):
        z = [z[p] + _bdot(x[p], z[p]) for p in pairs]
        if lvl + 1 < steps:
            x = [_bdot(x[p], x[p]) for p in pairs]
    s0 = [s_ref[p] for p in pairs]
    us = [_bdot_nt(z[p][:, :LANES], s0[p]) + z[p][:, LANES:] for p in pairs]
    ys = [_bdot_nt(rts[p], s0[p]) + _bdot(aa[p][c2:, :c2], us[p]) + _bdot(aa[p][c2:, c2:], vs[p]) for p in pairs]
    for p in pairs:
        e_rel = jnp.exp(g_last[p] - g[p])
        s_ref[p] = s0[p] * jnp.exp(g_last[p]) + _bdot_tn(
            jnp.concatenate([us[p], vs[p]], axis=0),
            jnp.concatenate([stack(bb_ref[0, p] * e_rel), stack(k_ref[0, p] * e_rel)], axis=0))
    for p in pairs:
        y = ys[p][:CHUNK] + ys[p][CHUNK:]
        mean = _seg_sum(y, segm) * (1.0 / RWKV_HEAD)
        d = y - mean
        var = _seg_sum(d * d, segm) * (1.0 / RWKV_HEAD)
        out = d * lax.rsqrt(var + GN_EPS) * gnw_ref[p] + gnb_ref[p] + bonus_ref[0, p]
        o_ref[0, :, p * LANES:(p + 1) * LANES] = out.astype(o_ref.dtype)


def _rwkv_scan(r, g, k, v, kk, bb, bonus, gnw, gnb, segm):
    b, _, s, _ = r.shape
    pm = pl.BlockSpec((1, N_PAIRS, CHUNK, LANES), lambda bi, c: (bi, 0, c, 0))
    full = lambda a: pl.BlockSpec(a.shape, lambda bi, c: (0,) * a.ndim)
    return pl.pallas_call(
        _rwkv_scan_kernel,
        grid=(b, s // CHUNK),
        in_specs=[pm] * 7 + [full(gnw), full(gnb), full(segm)],
        out_specs=pl.BlockSpec((1, CHUNK, MIX_WIDTH), lambda bi, c: (bi, c, 0)),
        out_shape=jax.ShapeDtypeStruct((b, s, MIX_WIDTH), BF16),
        scratch_shapes=[pltpu.VMEM((N_PAIRS, LANES, LANES), F32)],
        compiler_params=_cparams(("parallel", "arbitrary")),
        name="rwkv_scan",
    )(r, g, k, v, kk, bb, bonus, gnw, gnb, segm)


def _out_kernel(x_ref, mix_ref, qm_ref, gate_ref, mk_ref, mv_ref, wo_ref, o_ref):
    gate = gate_ref[0].astype(F32)
    sg = gate * (1.0 / (1.0 + jnp.exp(-gate)))
    acc = x_ref[0] + _bdot(mix_ref[0].astype(F32) * sg[:, :MIX_WIDTH], wo_ref[:MIX_WIDTH, :])
    heads = []
    for h in range(X_HEADS):
        sl = slice(h * X_HEAD_DIM, (h + 1) * X_HEAD_DIM)
        s = lax.dot_general(qm_ref[0][:, sl], mk_ref[0][:, sl], (((1,), (1,)), ((), ())),
                            preferred_element_type=F32) * (X_HEAD_DIM ** -0.5)
        p = jnp.exp(s - jnp.max(s, axis=-1, keepdims=True))
        p = p / jnp.sum(p, axis=-1, keepdims=True)
        heads.append(jnp.dot(p.astype(BF16), mv_ref[0][:, sl], preferred_element_type=F32))
    mem_out = jnp.concatenate(heads, axis=1)
    acc = acc + _bdot(mem_out * sg[:, MIX_WIDTH:], wo_ref[MIX_WIDTH:, :])
    o_ref[0] = acc


def _out(x, mix, qm, gate, mk, mv, wo, ts):
    b, s, d = x.shape
    n_mem = mk.shape[1]
    tok = lambda w: pl.BlockSpec((1, ts, w), lambda bi, i: (bi, i, 0))
    memspec = pl.BlockSpec((1, n_mem, X_WIDTH), lambda bi, i: (bi, 0, 0))
    return pl.pallas_call(
        _out_kernel,
        grid=(b, s // ts),
        in_specs=[tok(d), tok(MIX_WIDTH), tok(X_WIDTH), tok(INNER), memspec, memspec,
                  pl.BlockSpec(wo.shape, lambda bi, i: (0, 0))],
        out_specs=tok(d),
        out_shape=jax.ShapeDtypeStruct((b, s, d), F32),
        compiler_params=_cparams(("parallel", "parallel")),
        name="out_proj",
    )(x, mix, qm, gate, mk, mv, wo)


def _final_norm_kernel(x_ref, g_ref, o_ref):
    x = x_ref[...]
    ms = jnp.mean(x * x, axis=-1, keepdims=True)
    o_ref[...] = x * lax.rsqrt(ms + NORM_EPS) * g_ref[...]


def _final_norm(x2d, g, tm):
    m, d = x2d.shape
    return pl.pallas_call(
        _final_norm_kernel,
        grid=(m // tm,),
        in_specs=[pl.BlockSpec((tm, d), lambda i: (i, 0)), pl.BlockSpec((1, d), lambda i: (0, 0))],
        out_specs=pl.BlockSpec((tm, d), lambda i: (i, 0)),
        out_shape=jax.ShapeDtypeStruct((m, d), F32),
        compiler_params=_cparams(("parallel",)),
        name="final_norm",
    )(x2d, g.reshape(1, d))


def _rope_perm():
    return np.concatenate([np.arange(0, QK_ROPE, 2), np.arange(1, QK_ROPE, 2)])


def _pack_mla_weights(w_in, w_uq, w_ukv):
    cq_w, ckv_w, kr_w, qm_w, gate_w = jnp.split(
        w_in, np.cumsum([Q_LORA, KV_LORA, QK_ROPE, X_WIDTH]).tolist(), axis=1)
    perm = _rope_perm()
    kr_w = jnp.pad(kr_w[:, perm], ((0, 0), (0, LANES - QK_ROPE)))
    wq = w_uq.reshape(Q_LORA, MLA_HEADS, QK_NOPE + QK_ROPE)
    wq = jnp.concatenate([wq[:, :, :QK_NOPE], wq[:, :, QK_NOPE:][:, :, perm],
                          jnp.zeros((Q_LORA, MLA_HEADS, QK_PACK - QK_NOPE - QK_ROPE), w_uq.dtype)], axis=2)
    wqt = wq.reshape(Q_LORA, MLA_HEADS * QK_PACK).T
    wkv = w_ukv.reshape(KV_LORA, MLA_HEADS, QK_NOPE + V_HEAD)
    wk = wkv[:, :, :QK_NOPE].reshape(KV_LORA, MLA_HEADS * QK_NOPE)
    wvt = wkv[:, :, QK_NOPE:].reshape(KV_LORA, MLA_HEADS * V_HEAD).T
    bf = lambda a: a.astype(BF16)
    return [bf(cq_w), bf(ckv_w), bf(kr_w), bf(qm_w), bf(gate_w)], bf(wqt), bf(wk), bf(wvt)


def _rope_freqs():
    inv_freq = ROPE_THETA ** (-jnp.arange(0, QK_ROPE, 2, dtype=F32) / QK_ROPE)
    lanes = jnp.concatenate([inv_freq, inv_freq, jnp.zeros((LANES - QK_ROPE,), F32)]).reshape(1, LANES)
    return lanes, inv_freq.reshape(HALF_ROPE, 1)


def _chunk_tri(ts):
    t = np.arange(ts)
    return jnp.asarray((t[:, None] // CHUNK == t[None, :] // CHUNK) & (t[None, :] <= t[:, None]), BF16)


def _head_seg():
    t = np.arange(LANES)
    return jnp.asarray(t[:, None] // RWKV_HEAD == t[None, :] // RWKV_HEAD, BF16)


def kernel(x, mem, positions, norm_g, mem_norm_g, w_mem_kv, w_in_mla, mla_q_norm_g, mla_kv_norm_g, mla_w_uq,
           mla_w_ukv, w_in_rwkv, rwkv_mu, rwkv_w0, rwkv_w2, rwkv_a0, rwkv_a2, rwkv_k_k, rwkv_k_a, rwkv_r_k,
           rwkv_gn_w, rwkv_gn_b, w_out, final_g):
    b, s, d = x.shape
    n_mem = mem.shape[1]
    depth = norm_g.shape[0]
    t = b * s
    tm = min(512, s)
    posc = positions.reshape(b, s, 1).astype(F32)
    posr = positions.reshape(b, 1, s).astype(F32)
    invr, invc = _rope_freqs()
    segm = _head_seg()
    prep_ts = min(128, s)
    lmat = _chunk_tri(prep_ts)
    row = lambda a: a.reshape(1, -1).astype(F32)

    for i in range(depth):
        j = i // 2
        mk, mv = _norm_proj(mem.reshape(b * n_mem, d), mem_norm_g[i],
                            [w_mem_kv[i][:, :X_WIDTH].astype(BF16), w_mem_kv[i][:, X_WIDTH:].astype(BF16)],
                            [BF16, BF16], tm=min(256, b * n_mem))
        mk = mk.reshape(b, n_mem, X_WIDTH)
        mv = mv.reshape(b, n_mem, X_WIDTH)
        if i % 2 == 0:
            w_list, wqt, wk, wvt = _pack_mla_weights(w_in_mla[j], mla_w_uq[j], mla_w_ukv[j])
            cq, ckv, kr, qm, gate = _norm_proj(x.reshape(t, d), norm_g[i], w_list, [F32, F32, F32, BF16, BF16],
                                               tm=tm)
            sh = lambda a: a.reshape(b, s, -1)
            qt, k, vt = _mla_up(sh(cq), sh(ckv), sh(kr), posc, posr, row(mla_q_norm_g[j]), row(mla_kv_norm_g[j]),
                                wqt, wk, wvt, invr, invc, tm=min(MLA_UP_TILE, FLASH_TILE, s),
                                tq=min(FLASH_TILE, s))
            mix = _flash(qt, k, vt, groups=FLASH_GROUPS)
        else:
            w = w_in_rwkv[j]
            edges = np.cumsum([MIX_WIDTH, MIX_WIDTH, MIX_WIDTH, DECAY_LORA + ICLR_LORA, X_WIDTH]).tolist()
            w_list = [a.astype(BF16) for a in jnp.split(w, edges, axis=1)]
            ur, uk, uv, uwa, qm, gate = _norm_proj(x.reshape(t, d), norm_g[i], w_list, [BF16] * 6, tm=min(256, s))
            mu = rwkv_mu[j]
            mur, muk, muv, muwa = [row(a) for a in jnp.split(mu, edges[:3])]
            zeros = jnp.zeros((DECAY_LORA, MIX_WIDTH), F32)
            wl = jnp.concatenate([jnp.concatenate([rwkv_w2[j], zeros], axis=1),
                                  jnp.concatenate([zeros, rwkv_a2[j]], axis=1)], axis=0).astype(BF16)
            sh = lambda a: a.reshape(b, s, -1)
            r, g, kmod, vv, kk, bb, bonus = _rwkv_prep(
                sh(ur), sh(uk), sh(uv), sh(uwa), mur, muk, muv, muwa, row(rwkv_w0[j]), row(rwkv_a0[j]), wl,
                row(rwkv_k_k[j]), row(rwkv_k_a[j]), row(rwkv_r_k[j]), lmat, segm, ts=prep_ts)
            gnw = rwkv_gn_w[j].reshape(N_PAIRS, 1, LANES)
            gnb = rwkv_gn_b[j].reshape(N_PAIRS, 1, LANES)
            mix = _rwkv_scan(r, g, kmod, vv, kk, bb, bonus, gnw, gnb, segm)
        x = _out(x, mix, qm.reshape(b, s, -1), gate.reshape(b, s, -1), mk, mv, w_out[i].astype(BF16),
                 ts=min(512, s))
    return _final_norm(x.reshape(t, d), final_g, tm=tm).reshape(b, s, d)
```

```python
import functools
import math

import jax
import jax.numpy as jnp
import numpy as np
from jax import lax
from jax.experimental import pallas as pl
from jax.experimental.pallas import tpu as pltpu

F32 = jnp.float32
BF16 = jnp.bfloat16

D_MODEL = 1024
X_HEADS = 4
X_HEAD_DIM = 128
X_WIDTH = X_HEADS * X_HEAD_DIM
MLA_HEADS = 12
QK_NOPE = 128
QK_ROPE = 64
V_HEAD = 128
Q_LORA = 384
KV_LORA = 256
ROPE_THETA = 10000.0
MIX_WIDTH = MLA_HEADS * V_HEAD
INNER = MIX_WIDTH + X_WIDTH
RWKV_HEAD = 64
RWKV_HEADS = MIX_WIDTH // RWKV_HEAD
DECAY_LORA = 64
ICLR_LORA = 64
GN_EPS = 64e-5
NORM_EPS = 1e-6

LANES = 128
QK_PACK = 2 * LANES
N_PAIRS = MIX_WIDTH // LANES
CHUNK = 64
FLASH_TILE = 2048
FLASH_GROUPS = 8
MLA_UP_TILE = 512
BF16_SUBLANES = 16
V_AUG = V_HEAD + BF16_SUBLANES
VMEM_LIMIT = 56 * 1024 * 1024


def _cparams(sem):
    return pltpu.CompilerParams(dimension_semantics=sem, vmem_limit_bytes=VMEM_LIMIT)


def _bdot(a, b):
    return jnp.dot(a.astype(BF16), b.astype(BF16), preferred_element_type=F32)


def _bdot_nt(a, b):
    return lax.dot_general(a.astype(BF16), b.astype(BF16), (((1,), (1,)), ((), ())),
                           preferred_element_type=F32)


def _bdot_tn(a, b):
    return lax.dot_general(a.astype(BF16), b.astype(BF16), (((0,), (0,)), ((), ())),
                           preferred_element_type=F32)


def _split_dot(x, m_bf16, terms):
    acc = None
    rem = x
    for _ in range(terms):
        piece = rem.astype(BF16)
        part = jnp.dot(piece, m_bf16, preferred_element_type=F32)
        acc = part if acc is None else acc + part
        rem = rem - piece.astype(F32)
    return acc


def _norm_proj_kernel(x_ref, g_ref, *refs, n_out):
    w_refs, o_refs = refs[:n_out], refs[n_out:]
    x = x_ref[...].astype(F32)
    ms = jnp.mean(x * x, axis=-1, keepdims=True)
    xn = (x * lax.rsqrt(ms + NORM_EPS) * g_ref[...]).astype(BF16)
    for w_ref, o_ref in zip(w_refs, o_refs):
        o_ref[...] = jnp.dot(xn, w_ref[...], preferred_element_type=F32).astype(o_ref.dtype)


def _norm_proj(x2d, g, weights, out_dtypes, tm):
    m, k = x2d.shape
    n_out = len(weights)
    in_specs = [pl.BlockSpec((tm, k), lambda i: (i, 0)), pl.BlockSpec((1, k), lambda i: (0, 0))]
    in_specs += [pl.BlockSpec(w.shape, lambda i: (0, 0)) for w in weights]
    out_specs = [pl.BlockSpec((tm, w.shape[1]), lambda i: (i, 0)) for w in weights]
    out_shape = [jax.ShapeDtypeStruct((m, w.shape[1]), dt) for w, dt in zip(weights, out_dtypes)]
    return pl.pallas_call(
        functools.partial(_norm_proj_kernel, n_out=n_out),
        grid=(m // tm,),
        in_specs=in_specs,
        out_specs=out_specs,
        out_shape=out_shape,
        compiler_params=_cparams(("parallel",)),
        name="norm_proj",
    )(x2d, g.reshape(1, k).astype(F32), *weights)


HALF_ROPE = QK_ROPE // 2


def _mla_up_kernel(cq_ref, ckv_ref, kr_ref, posc_ref, posr_ref, gq_ref, gkv_ref, wqt_ref, wk_ref, wvt_ref,
                   invr_ref, invc_ref, qt_ref, k_ref, vt_ref, *, scale):
    def rms(x, g):
        ms = jnp.mean(x * x, axis=-1, keepdims=True)
        return (x * lax.rsqrt(ms + NORM_EPS) * g).astype(BF16)

    cqn = rms(cq_ref[0], gq_ref[...])
    ckvn = rms(ckv_ref[0], gkv_ref[...])

    ang = posc_ref[0] * invr_ref[...]
    lane = lax.broadcasted_iota(jnp.int32, ang.shape, 1)
    cosv, sinv = jnp.cos(ang), jnp.sin(ang)
    kr = kr_ref[0]
    kr = (kr * cosv
          + pltpu.roll(kr, LANES - HALF_ROPE, 1) * jnp.where(lane < HALF_ROPE, -sinv, 0.0)
          + pltpu.roll(kr, HALF_ROPE, 1) * jnp.where((lane >= HALF_ROPE) & (lane < QK_ROPE), sinv, 0.0))
    kr = kr.astype(BF16)
    kn = jnp.dot(ckvn, wk_ref[...], preferred_element_type=F32).astype(BF16)

    ang_t = invc_ref[...] * posr_ref[0]
    cos_t, sin_t = jnp.cos(ang_t), jnp.sin(ang_t)
    r0, r1, r2 = QK_NOPE, QK_NOPE + HALF_ROPE, QK_NOPE + QK_ROPE
    ones_row = (lax.broadcasted_iota(jnp.int32, (V_AUG - V_HEAD, ang_t.shape[1]), 0) == 0).astype(BF16)
    for h in range(MLA_HEADS):
        k_ref[0, h, :, :LANES] = kn[:, h * LANES:(h + 1) * LANES]
        k_ref[0, h, :, LANES:] = kr
        vt_ref[0, h, 0, :V_HEAD] = _bdot_nt(wvt_ref[h * V_HEAD:(h + 1) * V_HEAD, :], ckvn).astype(BF16)
        vt_ref[0, h, 0, V_HEAD:] = ones_row
        qt = _bdot_nt(wqt_ref[h * QK_PACK:(h + 1) * QK_PACK, :], cqn)
        x1, x2 = qt[r0:r1], qt[r1:r2]
        qt = jnp.concatenate([qt[:r0], x1 * cos_t - x2 * sin_t, x2 * cos_t + x1 * sin_t, qt[r2:]], axis=0)
        qt_ref[0, h, 0] = (qt * scale).astype(BF16)


def _mla_up(cq, ckv, kr, posc, posr, gq, gkv, wqt, wk, wvt, invr, invc, tm, tq):
    b, s, _ = cq.shape
    n, r = s // tq, tq // tm
    tok = lambda w: pl.BlockSpec((1, tm, w), lambda bi, i: (bi, i, 0))
    full = lambda a: pl.BlockSpec(a.shape, lambda bi, i: (0,) * a.ndim)
    scale = float(QK_NOPE + QK_ROPE) ** -0.5 * math.log2(math.e)
    return pl.pallas_call(
        functools.partial(_mla_up_kernel, scale=scale),
        grid=(b, s // tm),
        in_specs=[tok(Q_LORA), tok(KV_LORA), tok(LANES), tok(1),
                  pl.BlockSpec((1, 1, tm), lambda bi, i: (bi, 0, i)),
                  full(gq), full(gkv), full(wqt), full(wk), full(wvt), full(invr), full(invc)],
        out_specs=[pl.BlockSpec((1, MLA_HEADS, 1, QK_PACK, tm), lambda bi, i: (bi, 0, i // r, 0, i % r)),
                   pl.BlockSpec((1, MLA_HEADS, tm, QK_PACK), lambda bi, i: (bi, 0, i, 0)),
                   pl.BlockSpec((1, MLA_HEADS, 1, V_AUG, tm), lambda bi, i: (bi, 0, i // r, 0, i % r))],
        out_shape=[jax.ShapeDtypeStruct((b, MLA_HEADS, n, QK_PACK, tq), BF16),
                   jax.ShapeDtypeStruct((b, MLA_HEADS, s, QK_PACK), BF16),
                   jax.ShapeDtypeStruct((b, MLA_HEADS, n, V_AUG, tq), BF16)],
        compiler_params=_cparams(("parallel", "parallel")),
        name="mla_up",
    )(cq, ckv, kr, posc, posr, gq, gkv, wqt, wk, wvt, invr, invc)


def _flash_kernel(qt_ref, k_ref, vt_ref, o_ref, st_ref, p_ref, acc_ref, *, tq, groups):
    i = pl.program_id(2)
    gw = tq // groups
    every = tuple(range(groups))
    qts = [qt_ref[0, 0, 0, :, g * gw:(g + 1) * gw] for g in every]

    def scores(tile, d, which):
        mb = {}
        if which:
            kj = k_ref[0, 0, pl.ds(pl.multiple_of(tile * tq, tq) + d * gw, gw), :]
        for g in which:
            st = jnp.dot(kj, qts[g], preferred_element_type=F32)
            st_ref[d % 2, g] = st
            mb[g] = jnp.max(st, axis=0, keepdims=True)
        return mb

    def softmax(d, g, mb, m, masked):
        st = st_ref[d % 2, g]
        if masked:
            key = lax.broadcasted_iota(jnp.int32, st.shape, 0)
            qry = lax.broadcasted_iota(jnp.int32, st.shape, 1)
            st = jnp.where(key <= qry, st, -jnp.inf)
            mb = jnp.max(st, axis=0, keepdims=True)
        m_new = jnp.maximum(m, mb)
        p_ref[d % 2, g] = jnp.exp2(st - m_new).astype(BF16)
        return m_new, jnp.exp2(m - m_new)

    def accumulate(tile, d, g, alpha):
        vtj = vt_ref[0, 0, tile, :, d * gw:(d + 1) * gw]
        acc_ref[g] = alpha * acc_ref[g] + jnp.dot(vtj, p_ref[d % 2, g], preferred_element_type=F32)

    def step(tile, d, state, b_groups, a_groups, c_groups, masked_group):
        mb, m, alpha = state
        nxt = (tile, d + 1) if d + 1 < groups else (tile + 1, 0)
        prv = (tile, d - 1) if d > 0 else (jnp.maximum(tile - 1, 0), groups - 1)
        mb_next = scores(*nxt, a_groups)
        for g in c_groups:
            accumulate(*prv, g, alpha[g])
        m, alpha = list(m), list(alpha)
        for g in b_groups:
            m[g], alpha[g] = softmax(d, g, mb[g], m[g], masked=(g == masked_group))
        return [mb_next.get(g, mb[g]) for g in every], m, alpha

    def trip(u, state):
        for d in every:
            state = step(u, d, state, every, every, every, None)
        return state

    p_ref[(groups - 1) % 2] = jnp.zeros(p_ref.shape[1:], BF16)
    acc_ref[...] = jnp.zeros_like(acc_ref)
    mb0 = scores(0, 0, every)
    per_group = lambda v: [jnp.full((1, gw), v, F32) for _ in every]
    state = ([mb0[g] for g in every], per_group(-jnp.inf), per_group(1.0))
    state = lax.fori_loop(0, i, trip, state)

    for d in every:
        state = step(i, d, state, every[d:], every[d + 1:], every if d == 0 else every[d - 1:], d)
    alpha = state[2]
    accumulate(i, groups - 1, groups - 1, alpha[groups - 1])
    for g in every:
        acc = acc_ref[g]
        out = acc[:V_HEAD] / acc[V_HEAD:V_HEAD + 1]
        o_ref[0, g * gw:(g + 1) * gw, :] = out.T.astype(o_ref.dtype)


def _flash(qt, k, vt, groups):
    b, _, n, _, tq = qt.shape
    s = n * tq
    gw = tq // groups
    return pl.pallas_call(
        functools.partial(_flash_kernel, tq=tq, groups=groups),
        grid=(b, MLA_HEADS, n),
        in_specs=[pl.BlockSpec((1, 1, 1, QK_PACK, tq), lambda bi, h, i: (bi, h, i, 0, 0)),
                  pl.BlockSpec((1, 1, s, QK_PACK), lambda bi, h, i: (bi, h, 0, 0)),
                  pl.BlockSpec((1, 1, n, V_AUG, tq), lambda bi, h, i: (bi, h, 0, 0, 0))],
        out_specs=pl.BlockSpec((1, tq, V_HEAD), lambda bi, h, i: (bi, i, h)),
        out_shape=jax.ShapeDtypeStruct((b, s, MIX_WIDTH), BF16),
        scratch_shapes=[pltpu.VMEM((2, groups, gw, gw), F32), pltpu.VMEM((2, groups, gw, gw), BF16),
                        pltpu.VMEM((groups, V_AUG, gw), F32)],
        compiler_params=_cparams(("parallel", "parallel", "arbitrary")),
        name="flash",
    )(qt, k, vt)


def _seg_sum(x, segm):
    return _split_dot(x, segm, 2)


def _rwkv_prep_kernel(ur_ref, uk_ref, uv_ref, uwa_ref, pr_ref, pk_ref, pv_ref, pwa_ref,
                      mur_ref, muk_ref, muv_ref, muwa_ref, w0_ref, a0_ref, wl_ref, kkw_ref, kaw_ref, rkw_ref,
                      lmat_ref, segm_ref,
                      r_o, g_o, k_o, v_o, kk_o, bb_o, bonus_o, *, prev_rows):
    first = pl.program_id(1) == 0

    def mix(cur_ref, prev_ref, mu_ref):
        u = cur_ref[0].astype(F32)
        last = prev_ref[0][prev_rows - 1:prev_rows, :].astype(F32)
        last = jnp.where(first, 0.0, last)
        rid = lax.broadcasted_iota(jnp.int32, u.shape, 0)
        up = jnp.where(rid == 0, last, pltpu.roll(u, 1, 0))
        return u + (up - u) * mu_ref[...]

    xr = mix(ur_ref, pr_ref, mur_ref)
    xk = mix(uk_ref, pk_ref, muk_ref)
    xv = mix(uv_ref, pv_ref, muv_ref)
    xwa = mix(uwa_ref, pwa_ref, muwa_ref)

    lane = lax.broadcasted_iota(jnp.int32, xwa.shape, 1)
    z = jnp.where(lane < DECAY_LORA, jnp.tanh(xwa), xwa)
    lo = jnp.dot(z.astype(BF16), wl_ref[...], preferred_element_type=F32)
    lw = -math.exp(-0.5) / (1.0 + jnp.exp(-(w0_ref[...] + lo[:, :MIX_WIDTH])))
    a = 1.0 / (1.0 + jnp.exp(-(a0_ref[...] + lo[:, MIX_WIDTH:])))
    kkr = xk * kkw_ref[...]
    kmod = xk * (1.0 + (a - 1.0) * kaw_ref[...])
    rkk = xr * kmod * rkw_ref[...]
    lmat = lmat_ref[...]
    segm = segm_ref[...]
    for p in range(N_PAIRS):
        sl = slice(p * LANES, (p + 1) * LANES)
        n2 = _seg_sum(kkr[:, sl] * kkr[:, sl], segm)
        kk = kkr[:, sl] * lax.rsqrt(jnp.maximum(n2, 1e-24))
        r_o[0, p] = xr[:, sl].astype(r_o.dtype)
        g_o[0, p] = _split_dot_left(lmat, lw[:, sl])
        k_o[0, p] = kmod[:, sl].astype(k_o.dtype)
        v_o[0, p] = xv[:, sl].astype(v_o.dtype)
        kk_o[0, p] = kk.astype(kk_o.dtype)
        bb_o[0, p] = (kk * a[:, sl]).astype(bb_o.dtype)
        bonus_o[0, p] = (_seg_sum(rkk[:, sl], segm) * xv[:, sl]).astype(bonus_o.dtype)


def _split_dot_left(m_bf16, x):
    acc = None
    rem = x
    for _ in range(3):
        piece = rem.astype(BF16)
        part = jnp.dot(m_bf16, piece, preferred_element_type=F32)
        acc = part if acc is None else acc + part
        rem = rem - piece.astype(F32)
    return acc


def _rwkv_prep(ur, uk, uv, uwa, mur, muk, muv, muwa, w0, a0, wl, kkw, kaw, rkw, lmat, segm, ts):
    b, s, _ = ur.shape
    prev_rows = 16
    cur = lambda w: pl.BlockSpec((1, ts, w), lambda bi, i: (bi, i, 0))
    prev = lambda w: pl.BlockSpec((1, prev_rows, w),
                                  lambda bi, i: (bi, jnp.maximum(i * (ts // prev_rows) - 1, 0), 0))
    full = lambda a: pl.BlockSpec(a.shape, lambda bi, i: (0,) * a.ndim)
    pm = pl.BlockSpec((1, N_PAIRS, ts, LANES), lambda bi, i: (bi, 0, i, 0))
    small = [mur, muk, muv, muwa, w0, a0, wl, kkw, kaw, rkw, lmat, segm]
    return pl.pallas_call(
        functools.partial(_rwkv_prep_kernel, prev_rows=prev_rows),
        grid=(b, s // ts),
        in_specs=[cur(MIX_WIDTH), cur(MIX_WIDTH), cur(MIX_WIDTH), cur(LANES),
                  prev(MIX_WIDTH), prev(MIX_WIDTH), prev(MIX_WIDTH), prev(LANES)] + [full(a) for a in small],
        out_specs=[pm] * 7,
        out_shape=[jax.ShapeDtypeStruct((b, N_PAIRS, s, LANES), F32 if n == 1 else BF16) for n in range(7)],
        compiler_params=_cparams(("parallel", "parallel")),
        name="rwkv_prep",
    )(ur, uk, uv, uwa, ur, uk, uv, uwa, *small)


def _rwkv_scan_kernel(r_ref, g_ref, k_ref, v_ref, kk_ref, bb_ref, bonus_ref, gnw_ref, gnb_ref, segm_ref,
                      o_ref, s_ref):
    c2 = 2 * CHUNK

    @pl.when(pl.program_id(1) == 0)
    def _():
        s_ref[...] = jnp.zeros_like(s_ref)

    lane = lax.broadcasted_iota(jnp.int32, (CHUNK, LANES), 1)
    head0 = lane < RWKV_HEAD

    def stack(x):
        return jnp.concatenate([jnp.where(head0, x, 0.0), jnp.where(head0, 0.0, x)], axis=0)

    row = lax.broadcasted_iota(jnp.int32, (2 * c2, 2 * c2), 0)
    col = lax.broadcasted_iota(jnp.int32, (2 * c2, 2 * c2), 1)
    same_head = ((row // CHUNK) % 2) == ((col // CHUNK) % 2)
    tr, tc = row % CHUNK, col % CHUNK
    aa_mask = same_head & ((tc < tr) | ((row >= c2) & (tc == tr)))
    segm = segm_ref[...]

    pairs = range(N_PAIRS)
    g = [g_ref[0, p] for p in pairs]
    g_last = [gp[CHUNK - 1:CHUNK, :] for gp in g]
    first_row = lax.broadcasted_iota(jnp.int32, (CHUNK, LANES), 0) == 0
    g_prev = [jnp.where(first_row, 0.0, pltpu.roll(gp, 1, 0)) for gp in g]
    ats = [stack(-kk_ref[0, p] * jnp.exp(g_prev[p])) for p in pairs]
    rts = [stack(r_ref[0, p] * jnp.exp(g[p])) for p in pairs]
    vs = [stack(v_ref[0, p]) for p in pairs]
    aa = []
    for p in pairs:
        e_neg = jnp.exp(-g[p])
        bt, kt = bb_ref[0, p] * e_neg, k_ref[0, p] * e_neg
        a = _bdot_nt(jnp.concatenate([ats[p], rts[p]], axis=0), jnp.concatenate([bt, bt, kt, kt], axis=0))
        aa.append(jnp.where(aa_mask, a, 0.0))
    x = [a[:c2, :c2] for a in aa]
    z = [jnp.concatenate([ats[p], _bdot(aa[p][:c2, c2:], vs[p])], axis=1) for p in pairs]
    levels = round(math.log(CHUNK, 4))
    assert 4 ** levels == CHUNK
    for lvl in range(levels):
        x2 = [_bdot(x[p], x[p]) for p in pairs]
        if lvl + 1 < levels:
            x34 = [_bdot(x2[p], jnp.concatenate([x[p], x2[p]], axis=1)) for p in pairs]
            x3 = [m[:, :c2] for m in x34]
        else:
            x3 = [_bdot(x2[p], x[p]) for p in pairs]
        z = [z[p] + _bdot(x[p] + x2[p] + x3[p], z[p]) for p in pairs]
        if lvl + 1 < levels:
            x = [m[:, c2:] for m in x34]
    s0 = [s_ref[p] for p in pairs]
    uv = [jnp.concatenate([(_bdot_nt(z[p][:, :LANES], s0[p]) + z[p][:, LANES:]).astype(BF16),
                           vs[p].astype(BF16)], axis=0) for p in pairs]
    ys = [_bdot_nt(rts[p], s0[p]) + _bdot(aa[p][c2:, :], uv[p]) for p in pairs]
    for p in pairs:
        e_rel = jnp.exp(g_last[p] - g[p])
        s_ref[p] = s0[p] * jnp.exp(g_last[p]) + _bdot_tn(
            uv[p], jnp.concatenate([stack(bb_ref[0, p] * e_rel), stack(k_ref[0, p] * e_rel)], axis=0))
    seg_sum = lambda t: jnp.dot(t.astype(BF16), segm, preferred_element_type=F32)
    for p in pairs:
        y = ys[p][:CHUNK] + ys[p][CHUNK:]
        mean = seg_sum(y) * (1.0 / RWKV_HEAD)
        d = y - mean
        var = seg_sum(d * d) * (1.0 / RWKV_HEAD)
        out = d * lax.rsqrt(var + GN_EPS) * gnw_ref[p] + gnb_ref[p] + bonus_ref[0, p]
        o_ref[0, :, p * LANES:(p + 1) * LANES] = out.astype(o_ref.dtype)


def _rwkv_scan(r, g, k, v, kk, bb, bonus, gnw, gnb, segm):
    b, _, s, _ = r.shape
    pm = pl.BlockSpec((1, N_PAIRS, CHUNK, LANES), lambda bi, c: (bi, 0, c, 0))
    full = lambda a: pl.BlockSpec(a.shape, lambda bi, c: (0,) * a.ndim)
    return pl.pallas_call(
        _rwkv_scan_kernel,
        grid=(b, s // CHUNK),
        in_specs=[pm] * 7 + [full(gnw), full(gnb), full(segm)],
        out_specs=pl.BlockSpec((1, CHUNK, MIX_WIDTH), lambda bi, c: (bi, c, 0)),
        out_shape=jax.ShapeDtypeStruct((b, s, MIX_WIDTH), BF16),
        scratch_shapes=[pltpu.VMEM((N_PAIRS, LANES, LANES), F32)],
        compiler_params=_cparams(("parallel", "arbitrary")),
        name="rwkv_scan",
    )(r, g, k, v, kk, bb, bonus, gnw, gnb, segm)


def _out_kernel(x_ref, mix_ref, qm_ref, gate_ref, mk_ref, mv_ref, wo_ref, *rest):
    o_ref = rest[-1]
    gate = gate_ref[0].astype(F32)
    sg = gate * (1.0 / (1.0 + jnp.exp(-gate)))
    acc = x_ref[0] + _bdot(mix_ref[0].astype(F32) * sg[:, :MIX_WIDTH], wo_ref[:MIX_WIDTH, :])
    heads = []
    for h in range(X_HEADS):
        sl = slice(h * X_HEAD_DIM, (h + 1) * X_HEAD_DIM)
        s = lax.dot_general(qm_ref[0][:, sl], mk_ref[0][:, sl], (((1,), (1,)), ((), ())),
                            preferred_element_type=F32) * (X_HEAD_DIM ** -0.5)
        p = jnp.exp(s - jnp.max(s, axis=-1, keepdims=True))
        p = p / jnp.sum(p, axis=-1, keepdims=True)
        heads.append(jnp.dot(p.astype(BF16), mv_ref[0][:, sl], preferred_element_type=F32))
    mem_out = jnp.concatenate(heads, axis=1)
    acc = acc + _bdot(mem_out * sg[:, MIX_WIDTH:], wo_ref[MIX_WIDTH:, :])
    if len(rest) == 2:
        ms = jnp.mean(acc * acc, axis=-1, keepdims=True)
        acc = acc * lax.rsqrt(ms + NORM_EPS) * rest[0][...]
    o_ref[0] = acc


def _out(x, mix, qm, gate, mk, mv, wo, ts, final_g=None):
    b, s, d = x.shape
    n_mem = mk.shape[1]
    tok = lambda w: pl.BlockSpec((1, ts, w), lambda bi, i: (bi, i, 0))
    memspec = pl.BlockSpec((1, n_mem, X_WIDTH), lambda bi, i: (bi, 0, 0))
    const = lambda a: pl.BlockSpec(a.shape, lambda bi, i: (0, 0))
    extra = [] if final_g is None else [final_g.reshape(1, d).astype(F32)]
    return pl.pallas_call(
        _out_kernel,
        grid=(b, s // ts),
        in_specs=[tok(d), tok(MIX_WIDTH), tok(X_WIDTH), tok(INNER), memspec, memspec, const(wo)]
                 + [const(a) for a in extra],
        out_specs=tok(d),
        out_shape=jax.ShapeDtypeStruct((b, s, d), F32),
        compiler_params=_cparams(("parallel", "parallel")),
        name="out_proj",
    )(x, mix, qm, gate, mk, mv, wo, *extra)


def _rope_perm():
    return np.concatenate([np.arange(0, QK_ROPE, 2), np.arange(1, QK_ROPE, 2)])


def _pack_mla_weights(w_in, w_uq, w_ukv):
    cq_w, ckv_w, kr_w, qm_w, gate_w = jnp.split(
        w_in, np.cumsum([Q_LORA, KV_LORA, QK_ROPE, X_WIDTH]).tolist(), axis=1)
    perm = _rope_perm()
    kr_w = jnp.pad(kr_w[:, perm], ((0, 0), (0, LANES - QK_ROPE)))
    wq = w_uq.reshape(Q_LORA, MLA_HEADS, QK_NOPE + QK_ROPE)
    wq = jnp.concatenate([wq[:, :, :QK_NOPE], wq[:, :, QK_NOPE:][:, :, perm],
                          jnp.zeros((Q_LORA, MLA_HEADS, QK_PACK - QK_NOPE - QK_ROPE), w_uq.dtype)], axis=2)
    wqt = wq.reshape(Q_LORA, MLA_HEADS * QK_PACK).T
    wkv = w_ukv.reshape(KV_LORA, MLA_HEADS, QK_NOPE + V_HEAD)
    wk = wkv[:, :, :QK_NOPE].reshape(KV_LORA, MLA_HEADS * QK_NOPE)
    wvt = wkv[:, :, QK_NOPE:].reshape(KV_LORA, MLA_HEADS * V_HEAD).T
    bf = lambda a: a.astype(BF16)
    return [bf(cq_w), bf(ckv_w), bf(kr_w), bf(qm_w), bf(gate_w)], bf(wqt), bf(wk), bf(wvt)


def _rope_freqs():
    inv_freq = ROPE_THETA ** (-jnp.arange(0, QK_ROPE, 2, dtype=F32) / QK_ROPE)
    lanes = jnp.concatenate([inv_freq, inv_freq, jnp.zeros((LANES - QK_ROPE,), F32)]).reshape(1, LANES)
    return lanes, inv_freq.reshape(HALF_ROPE, 1)


def _chunk_tri(ts):
    t = np.arange(ts)
    return jnp.asarray((t[:, None] // CHUNK == t[None, :] // CHUNK) & (t[None, :] <= t[:, None]), BF16)


def _head_seg():
    t = np.arange(LANES)
    return jnp.asarray(t[:, None] // RWKV_HEAD == t[None, :] // RWKV_HEAD, BF16)


def kernel(x, mem, positions, norm_g, mem_norm_g, w_mem_kv, w_in_mla, mla_q_norm_g, mla_kv_norm_g, mla_w_uq,
           mla_w_ukv, w_in_rwkv, rwkv_mu, rwkv_w0, rwkv_w2, rwkv_a0, rwkv_a2, rwkv_k_k, rwkv_k_a, rwkv_r_k,
           rwkv_gn_w, rwkv_gn_b, w_out, final_g):
    b, s, d = x.shape
    n_mem = mem.shape[1]
    depth = norm_g.shape[0]
    t = b * s
    tm = min(512, s)
    posc = positions.reshape(b, s, 1).astype(F32)
    posr = positions.reshape(b, 1, s).astype(F32)
    invr, invc = _rope_freqs()
    segm = _head_seg()
    prep_ts = min(128, s)
    lmat = _chunk_tri(prep_ts)
    row = lambda a: a.reshape(1, -1).astype(F32)

    for i in range(depth):
        j = i // 2
        mk, mv = _norm_proj(mem.reshape(b * n_mem, d), mem_norm_g[i],
                            [w_mem_kv[i][:, :X_WIDTH].astype(BF16), w_mem_kv[i][:, X_WIDTH:].astype(BF16)],
                            [BF16, BF16], tm=min(256, b * n_mem))
        mk = mk.reshape(b, n_mem, X_WIDTH)
        mv = mv.reshape(b, n_mem, X_WIDTH)
        if i % 2 == 0:
            w_list, wqt, wk, wvt = _pack_mla_weights(w_in_mla[j], mla_w_uq[j], mla_w_ukv[j])
            cq, ckv, kr, qm, gate = _norm_proj(x.reshape(t, d), norm_g[i], w_list, [F32, F32, F32, BF16, BF16],
                                               tm=tm)
            sh = lambda a: a.reshape(b, s, -1)
            qt, k, vt = _mla_up(sh(cq), sh(ckv), sh(kr), posc, posr, row(mla_q_norm_g[j]), row(mla_kv_norm_g[j]),
                                wqt, wk, wvt, invr, invc, tm=min(MLA_UP_TILE, FLASH_TILE, s),
                                tq=min(FLASH_TILE, s))
            mix = _flash(qt, k, vt, groups=FLASH_GROUPS)
        else:
            w = w_in_rwkv[j]
            edges = np.cumsum([MIX_WIDTH, MIX_WIDTH, MIX_WIDTH, DECAY_LORA + ICLR_LORA, X_WIDTH]).tolist()
            w_list = [a.astype(BF16) for a in jnp.split(w, edges, axis=1)]
            ur, uk, uv, uwa, qm, gate = _norm_proj(x.reshape(t, d), norm_g[i], w_list, [BF16] * 6, tm=min(256, s))
            mu = rwkv_mu[j]
            mur, muk, muv, muwa = [row(a) for a in jnp.split(mu, edges[:3])]
            zeros = jnp.zeros((DECAY_LORA, MIX_WIDTH), F32)
            wl = jnp.concatenate([jnp.concatenate([rwkv_w2[j], zeros], axis=1),
                                  jnp.concatenate([zeros, rwkv_a2[j]], axis=1)], axis=0).astype(BF16)
            sh = lambda a: a.reshape(b, s, -1)
            r, g, kmod, vv, kk, bb, bonus = _rwkv_prep(
                sh(ur), sh(uk), sh(uv), sh(uwa), mur, muk, muv, muwa, row(rwkv_w0[j]), row(rwkv_a0[j]), wl,
                row(rwkv_k_k[j]), row(rwkv_k_a[j]), row(rwkv_r_k[j]), lmat, segm, ts=prep_ts)
            gnw = rwkv_gn_w[j].reshape(N_PAIRS, 1, LANES)
            gnb = rwkv_gn_b[j].reshape(N_PAIRS, 1, LANES)
            mix = _rwkv_scan(r, g, kmod, vv, kk, bb, bonus, gnw, gnb, segm)
        x = _out(x, mix, qm.reshape(b, s, -1), gate.reshape(b, s, -1), mk, mv, w_out[i].astype(BF16),
                 ts=min(512, s), final_g=final_g if i == depth - 1 else None)
    return x
```

```python
import functools
import math

import jax
import jax.numpy as jnp
import numpy as np
from jax import lax
from jax.experimental import pallas as pl
from jax.experimental.pallas import tpu as pltpu

F32 = jnp.float32
BF16 = jnp.bfloat16

D_MODEL = 1024
X_HEADS = 4
X_HEAD_DIM = 128
X_WIDTH = X_HEADS * X_HEAD_DIM
MLA_HEADS = 12
QK_NOPE = 128
QK_ROPE = 64
V_HEAD = 128
Q_LORA = 384
KV_LORA = 256
ROPE_THETA = 10000.0
MIX_WIDTH = MLA_HEADS * V_HEAD
INNER = MIX_WIDTH + X_WIDTH
RWKV_HEAD = 64
RWKV_HEADS = MIX_WIDTH // RWKV_HEAD
DECAY_LORA = 64
ICLR_LORA = 64
GN_EPS = 64e-5
NORM_EPS = 1e-6

LANES = 128
QK_PACK = 2 * LANES
N_PAIRS = MIX_WIDTH // LANES
CHUNK = 64
SCAN_BATCH = 2
FLASH_TILE = 2048
FLASH_GROUPS = 8
MLA_UP_TILE = 512
BF16_SUBLANES = 16
V_AUG = V_HEAD + BF16_SUBLANES
VMEM_LIMIT = 56 * 1024 * 1024


def _cparams(sem):
    return pltpu.CompilerParams(dimension_semantics=sem, vmem_limit_bytes=VMEM_LIMIT)


def _bdot(a, b):
    return jnp.dot(a.astype(BF16), b.astype(BF16), preferred_element_type=F32)


def _bdot_nt(a, b):
    return lax.dot_general(a.astype(BF16), b.astype(BF16), (((1,), (1,)), ((), ())),
                           preferred_element_type=F32)


def _bdot_tn(a, b):
    return lax.dot_general(a.astype(BF16), b.astype(BF16), (((0,), (0,)), ((), ())),
                           preferred_element_type=F32)


def _split_dot(x, m_bf16, terms):
    acc = None
    rem = x
    for _ in range(terms):
        piece = rem.astype(BF16)
        part = jnp.dot(piece, m_bf16, preferred_element_type=F32)
        acc = part if acc is None else acc + part
        rem = rem - piece.astype(F32)
    return acc


def _norm_proj_kernel(x_ref, g_ref, *refs, n_out):
    w_refs, o_refs = refs[:n_out], refs[n_out:]
    x = x_ref[...].astype(F32)
    ms = jnp.mean(x * x, axis=-1, keepdims=True)
    xn = (x * lax.rsqrt(ms + NORM_EPS) * g_ref[...]).astype(BF16)
    for w_ref, o_ref in zip(w_refs, o_refs):
        o_ref[...] = jnp.dot(xn, w_ref[...], preferred_element_type=F32).astype(o_ref.dtype)


def _norm_proj(x2d, g, weights, out_dtypes, tm):
    m, k = x2d.shape
    n_out = len(weights)
    in_specs = [pl.BlockSpec((tm, k), lambda i: (i, 0)), pl.BlockSpec((1, k), lambda i: (0, 0))]
    in_specs += [pl.BlockSpec(w.shape, lambda i: (0, 0)) for w in weights]
    out_specs = [pl.BlockSpec((tm, w.shape[1]), lambda i: (i, 0)) for w in weights]
    out_shape = [jax.ShapeDtypeStruct((m, w.shape[1]), dt) for w, dt in zip(weights, out_dtypes)]
    return pl.pallas_call(
        functools.partial(_norm_proj_kernel, n_out=n_out),
        grid=(m // tm,),
        in_specs=in_specs,
        out_specs=out_specs,
        out_shape=out_shape,
        compiler_params=_cparams(("parallel",)),
        name="norm_proj",
    )(x2d, g.reshape(1, k).astype(F32), *weights)


HALF_ROPE = QK_ROPE // 2


def _mla_up_kernel(cq_ref, ckv_ref, kr_ref, posc_ref, posr_ref, gq_ref, gkv_ref, wqt_ref, wk_ref, wvt_ref,
                   invr_ref, invc_ref, qt_ref, k_ref, vt_ref, *, scale):
    def rms(x, g):
        ms = jnp.mean(x * x, axis=-1, keepdims=True)
        return (x * lax.rsqrt(ms + NORM_EPS) * g).astype(BF16)

    cqn = rms(cq_ref[0], gq_ref[...])
    ckvn = rms(ckv_ref[0], gkv_ref[...])

    ang = posc_ref[0] * invr_ref[...]
    lane = lax.broadcasted_iota(jnp.int32, ang.shape, 1)
    cosv, sinv = jnp.cos(ang), jnp.sin(ang)
    kr = kr_ref[0]
    kr = (kr * cosv
          + pltpu.roll(kr, LANES - HALF_ROPE, 1) * jnp.where(lane < HALF_ROPE, -sinv, 0.0)
          + pltpu.roll(kr, HALF_ROPE, 1) * jnp.where((lane >= HALF_ROPE) & (lane < QK_ROPE), sinv, 0.0))
    kr = kr.astype(BF16)
    kn = jnp.dot(ckvn, wk_ref[...], preferred_element_type=F32).astype(BF16)

    ang_t = invc_ref[...] * posr_ref[0]
    cos_t, sin_t = jnp.cos(ang_t), jnp.sin(ang_t)
    r0, r1, r2 = QK_NOPE, QK_NOPE + HALF_ROPE, QK_NOPE + QK_ROPE
    ones_row = (lax.broadcasted_iota(jnp.int32, (V_AUG - V_HEAD, ang_t.shape[1]), 0) == 0).astype(BF16)
    for h in range(MLA_HEADS):
        k_ref[0, h, :, :LANES] = kn[:, h * LANES:(h + 1) * LANES]
        k_ref[0, h, :, LANES:] = kr
        vt_ref[0, h, 0, :V_HEAD] = _bdot_nt(wvt_ref[h * V_HEAD:(h + 1) * V_HEAD, :], ckvn).astype(BF16)
        vt_ref[0, h, 0, V_HEAD:] = ones_row
        qt = _bdot_nt(wqt_ref[h * QK_PACK:(h + 1) * QK_PACK, :], cqn)
        x1, x2 = qt[r0:r1], qt[r1:r2]
        qt = jnp.concatenate([qt[:r0], x1 * cos_t - x2 * sin_t, x2 * cos_t + x1 * sin_t, qt[r2:]], axis=0)
        qt_ref[0, h, 0] = (qt * scale).astype(BF16)


def _mla_up(cq, ckv, kr, posc, posr, gq, gkv, wqt, wk, wvt, invr, invc, tm, tq):
    b, s, _ = cq.shape
    n, r = s // tq, tq // tm
    tok = lambda w: pl.BlockSpec((1, tm, w), lambda bi, i: (bi, i, 0))
    full = lambda a: pl.BlockSpec(a.shape, lambda bi, i: (0,) * a.ndim)
    scale = float(QK_NOPE + QK_ROPE) ** -0.5 * math.log2(math.e)
    return pl.pallas_call(
        functools.partial(_mla_up_kernel, scale=scale),
        grid=(b, s // tm),
        in_specs=[tok(Q_LORA), tok(KV_LORA), tok(LANES), tok(1),
                  pl.BlockSpec((1, 1, tm), lambda bi, i: (bi, 0, i)),
                  full(gq), full(gkv), full(wqt), full(wk), full(wvt), full(invr), full(invc)],
        out_specs=[pl.BlockSpec((1, MLA_HEADS, 1, QK_PACK, tm), lambda bi, i: (bi, 0, i // r, 0, i % r)),
                   pl.BlockSpec((1, MLA_HEADS, tm, QK_PACK), lambda bi, i: (bi, 0, i, 0)),
                   pl.BlockSpec((1, MLA_HEADS, 1, V_AUG, tm), lambda bi, i: (bi, 0, i // r, 0, i % r))],
        out_shape=[jax.ShapeDtypeStruct((b, MLA_HEADS, n, QK_PACK, tq), BF16),
                   jax.ShapeDtypeStruct((b, MLA_HEADS, s, QK_PACK), BF16),
                   jax.ShapeDtypeStruct((b, MLA_HEADS, n, V_AUG, tq), BF16)],
        compiler_params=_cparams(("parallel", "parallel")),
        name="mla_up",
    )(cq, ckv, kr, posc, posr, gq, gkv, wqt, wk, wvt, invr, invc)


def _flash_kernel(qt_ref, k_ref, vt_ref, o_ref, st_ref, p_ref, acc_ref, *, tq, groups):
    i = pl.program_id(2)
    gw = tq // groups
    every = tuple(range(groups))
    qts = [qt_ref[0, 0, 0, :, g * gw:(g + 1) * gw] for g in every]

    def scores(tile, d, which):
        mb = {}
        if which:
            kj = k_ref[0, 0, pl.ds(pl.multiple_of(tile * tq, tq) + d * gw, gw), :]
        for g in which:
            st = jnp.dot(kj, qts[g], preferred_element_type=F32)
            st_ref[d % 2, g] = st
            mb[g] = jnp.max(st, axis=0, keepdims=True)
        return mb

    def softmax(d, g, mb, m, masked):
        st = st_ref[d % 2, g]
        if masked:
            key = lax.broadcasted_iota(jnp.int32, st.shape, 0)
            qry = lax.broadcasted_iota(jnp.int32, st.shape, 1)
            st = jnp.where(key <= qry, st, -jnp.inf)
            mb = jnp.max(st, axis=0, keepdims=True)
        m_new = jnp.maximum(m, mb)
        p_ref[d % 2, g] = jnp.exp2(st - m_new).astype(BF16)
        return m_new, jnp.exp2(m - m_new)

    def accumulate(tile, d, g, alpha):
        vtj = vt_ref[0, 0, tile, :, d * gw:(d + 1) * gw]
        acc_ref[g] = alpha * acc_ref[g] + jnp.dot(vtj, p_ref[d % 2, g], preferred_element_type=F32)

    def step(tile, d, state, b_groups, a_groups, c_groups, masked_group):
        mb, m, alpha = state
        nxt = (tile, d + 1) if d + 1 < groups else (tile + 1, 0)
        prv = (tile, d - 1) if d > 0 else (jnp.maximum(tile - 1, 0), groups - 1)
        mb_next = scores(*nxt, a_groups)
        for g in c_groups:
            accumulate(*prv, g, alpha[g])
        m, alpha = list(m), list(alpha)
        for g in b_groups:
            m[g], alpha[g] = softmax(d, g, mb[g], m[g], masked=(g == masked_group))
        return [mb_next.get(g, mb[g]) for g in every], m, alpha

    def trip(u, state):
        for d in every:
            state = step(u, d, state, every, every, every, None)
        return state

    p_ref[(groups - 1) % 2] = jnp.zeros(p_ref.shape[1:], BF16)
    acc_ref[...] = jnp.zeros_like(acc_ref)
    mb0 = scores(0, 0, every)
    per_group = lambda v: [jnp.full((1, gw), v, F32) for _ in every]
    state = ([mb0[g] for g in every], per_group(-jnp.inf), per_group(1.0))
    state = lax.fori_loop(0, i, trip, state)

    for d in every:
        state = step(i, d, state, every[d:], every[d + 1:], every if d == 0 else every[d - 1:], d)
    alpha = state[2]
    accumulate(i, groups - 1, groups - 1, alpha[groups - 1])
    for g in every:
        acc = acc_ref[g]
        out = acc[:V_HEAD] / acc[V_HEAD:V_HEAD + 1]
        o_ref[0, g * gw:(g + 1) * gw, :] = out.T.astype(o_ref.dtype)


def _flash(qt, k, vt, groups):
    b, _, n, _, tq = qt.shape
    s = n * tq
    gw = tq // groups
    return pl.pallas_call(
        functools.partial(_flash_kernel, tq=tq, groups=groups),
        grid=(b, MLA_HEADS, n),
        in_specs=[pl.BlockSpec((1, 1, 1, QK_PACK, tq), lambda bi, h, i: (bi, h, i, 0, 0)),
                  pl.BlockSpec((1, 1, s, QK_PACK), lambda bi, h, i: (bi, h, 0, 0)),
                  pl.BlockSpec((1, 1, n, V_AUG, tq), lambda bi, h, i: (bi, h, 0, 0, 0))],
        out_specs=pl.BlockSpec((1, tq, V_HEAD), lambda bi, h, i: (bi, i, h)),
        out_shape=jax.ShapeDtypeStruct((b, s, MIX_WIDTH), BF16),
        scratch_shapes=[pltpu.VMEM((2, groups, gw, gw), F32), pltpu.VMEM((2, groups, gw, gw), BF16),
                        pltpu.VMEM((groups, V_AUG, gw), F32)],
        compiler_params=_cparams(("parallel", "parallel", "arbitrary")),
        name="flash",
    )(qt, k, vt)


def _split_dot_left(m_bf16, x):
    acc = None
    rem = x
    for _ in range(2):
        piece = rem.astype(BF16)
        part = jnp.dot(m_bf16, piece, preferred_element_type=F32)
        acc = part if acc is None else acc + part
        rem = rem - piece.astype(F32)
    return acc


RWKV_IN_TILE = 256
SLAB = 2 * LANES


def _rwkv_in_kernel(x_ref, g_ref, wr_ref, wk_ref, wv_ref, wwa_ref, wqm_ref, wgate_ref,
                    mur_ref, muk_ref, muv_ref, muwa_ref, w0_ref, a0_ref, wl_ref, kkw_ref, kaw_ref, rkw_ref,
                    lmat_ref, segm_ref,
                    r_o, g_o, k_o, v_o, kk_o, bb_o, bonus_o, qm_o, gate_o,
                    lr_ref, lk_ref, lv_ref, lwa_ref):
    tm = x_ref.shape[1]

    @pl.when(pl.program_id(1) == 0)
    def _():
        for ref in (lr_ref, lk_ref, lv_ref, lwa_ref):
            ref[...] = jnp.zeros_like(ref)

    x = x_ref[0]
    ms = jnp.mean(x * x, axis=-1, keepdims=True)
    xn = (x * lax.rsqrt(ms + NORM_EPS) * g_ref[...]).astype(BF16)
    rid = lax.broadcasted_iota(jnp.int32, (tm, 1), 0)

    def mix(u, last_ref, sl, mu):
        up = jnp.where(rid == 0, last_ref[:, sl], pltpu.roll(u, 1, 0))
        last_ref[:, sl] = u[tm - 1:tm, :]
        return u + (up - u) * mu

    def project(q):
        sl = slice(q * SLAB, (q + 1) * SLAB)
        return tuple(jnp.dot(xn, w_ref[:, sl], preferred_element_type=F32) for w_ref in (wr_ref, wk_ref, wv_ref))

    xwa = mix(jnp.dot(xn, wwa_ref[...], preferred_element_type=F32), lwa_ref, slice(None), muwa_ref[...])
    lane = lax.broadcasted_iota(jnp.int32, xwa.shape, 1)
    z = jnp.where(lane < DECAY_LORA, jnp.tanh(xwa), xwa)
    lo = jnp.dot(z.astype(BF16), wl_ref[...], preferred_element_type=F32)
    lmat = lmat_ref[...]
    segm = segm_ref[...]

    def prepare(q, u):
        sl = slice(q * SLAB, (q + 1) * SLAB)
        xr, xk, xv = (mix(ue, last_ref, sl, mu_ref[:, sl])
                      for ue, last_ref, mu_ref in zip(u, (lr_ref, lk_ref, lv_ref), (mur_ref, muk_ref, muv_ref)))
        lw = -math.exp(-0.5) / (1.0 + jnp.exp(-(w0_ref[:, sl] + lo[:, sl])))
        a = 1.0 / (1.0 + jnp.exp(-(a0_ref[:, sl] + lo[:, MIX_WIDTH + q * SLAB:MIX_WIDTH + (q + 1) * SLAB])))
        kkr = xk * kkw_ref[:, sl]
        kmod = xk * (1.0 + (a - 1.0) * kaw_ref[:, sl])
        n2 = _split_dot(kkr * kkr, segm, 1)
        kk = kkr * lax.rsqrt(jnp.maximum(n2, 1e-24))
        bonus = _split_dot(xr * kmod * rkw_ref[:, sl], segm, 1) * xv
        gcum = _split_dot_left(lmat, lw)
        for half in range(SLAB // LANES):
            p = q * (SLAB // LANES) + half
            hl = slice(half * LANES, (half + 1) * LANES)
            r_o[0, p] = xr[:, hl].astype(r_o.dtype)
            g_o[0, p] = gcum[:, hl]
            k_o[0, p] = kmod[:, hl].astype(k_o.dtype)
            v_o[0, p] = xv[:, hl].astype(v_o.dtype)
            kk_o[0, p] = kk[:, hl].astype(kk_o.dtype)
            bb_o[0, p] = (kk * a)[:, hl].astype(bb_o.dtype)
            bonus_o[0, p] = bonus[:, hl].astype(bonus_o.dtype)

    n_slabs = MIX_WIDTH // SLAB
    pending = project(0)
    for q in range(n_slabs):
        nxt = project(q + 1) if q + 1 < n_slabs else None
        if q + 1 == n_slabs:
            qm_o[0] = jnp.dot(xn, wqm_ref[...], preferred_element_type=F32).astype(qm_o.dtype)
            gate_o[0] = jnp.dot(xn, wgate_ref[...], preferred_element_type=F32).astype(gate_o.dtype)
        prepare(q, pending)
        pending = nxt


def _rwkv_in(x, g, weights, mus, w0, a0, wl, kkw, kaw, rkw, lmat, segm, tm):
    b, s, d = x.shape
    wr, wk, wv, wwa, wqm, wgate = weights
    cur = pl.BlockSpec((1, tm, d), lambda bi, i: (bi, i, 0))
    const = lambda a: pl.BlockSpec(a.shape, lambda bi, i: (0,) * a.ndim, pipeline_mode=pl.Buffered(1))
    pm = pl.BlockSpec((1, N_PAIRS, tm, LANES), lambda bi, i: (bi, 0, i, 0))
    tok = lambda w: pl.BlockSpec((1, tm, w), lambda bi, i: (bi, i, 0))
    small = [g.reshape(1, d).astype(F32), wr, wk, wv, wwa, wqm, wgate, *mus, w0, a0, wl, kkw, kaw, rkw, lmat, segm]
    pair_shape = lambda dt: jax.ShapeDtypeStruct((b, N_PAIRS, s, LANES), dt)
    return pl.pallas_call(
        _rwkv_in_kernel,
        grid=(b, s // tm),
        in_specs=[cur] + [const(a) for a in small],
        out_specs=[pm] * 7 + [tok(X_WIDTH), tok(INNER)],
        out_shape=[pair_shape(F32 if n == 1 else BF16) for n in range(7)]
                  + [jax.ShapeDtypeStruct((b, s, X_WIDTH), BF16), jax.ShapeDtypeStruct((b, s, INNER), BF16)],
        scratch_shapes=[pltpu.VMEM((1, MIX_WIDTH), F32)] * 3 + [pltpu.VMEM((1, LANES), F32)],
        compiler_params=_cparams(("parallel", "arbitrary")),
        name="rwkv_in",
    )(x, *small)


def _rwkv_scan_kernel(r_ref, g_ref, k_ref, v_ref, kk_ref, bb_ref, bonus_ref, gnw_ref, gnb_ref, segm_ref,
                      o_ref, s_ref):
    c2 = 2 * CHUNK

    @pl.when(pl.program_id(1) == 0)
    def _():
        s_ref[...] = jnp.zeros_like(s_ref)

    lane = lax.broadcasted_iota(jnp.int32, (CHUNK, LANES), 1)
    head0 = lane < RWKV_HEAD

    def stack(x):
        return jnp.concatenate([jnp.where(head0, x, 0.0), jnp.where(head0, 0.0, x)], axis=0)

    row = lax.broadcasted_iota(jnp.int32, (2 * c2, 2 * c2), 0)
    col = lax.broadcasted_iota(jnp.int32, (2 * c2, 2 * c2), 1)
    same_head = ((row // CHUNK) % 2) == ((col // CHUNK) % 2)
    tr, tc = row % CHUNK, col % CHUNK
    aa_mask = same_head & ((tc < tr) | ((row >= c2) & (tc == tr)))
    segm = segm_ref[...]

    units = [(bi, p) for bi in range(r_ref.shape[0]) for p in range(N_PAIRS)]
    pairs = range(len(units))
    g = [g_ref[u] for u in units]
    g_last = [gp[CHUNK - 1:CHUNK, :] for gp in g]
    first_row = lax.broadcasted_iota(jnp.int32, (CHUNK, LANES), 0) == 0
    g_prev = [jnp.where(first_row, 0.0, pltpu.roll(gp, 1, 0)) for gp in g]
    ats = [stack(-kk_ref[units[p]] * jnp.exp(g_prev[p])) for p in pairs]
    rts = [stack(r_ref[units[p]] * jnp.exp(g[p])) for p in pairs]
    vs = [stack(v_ref[u]) for u in units]
    aa = []
    for p in pairs:
        e_neg = jnp.exp(-g[p])
        bt, kt = bb_ref[units[p]] * e_neg, k_ref[units[p]] * e_neg
        a = _bdot_nt(jnp.concatenate([ats[p], rts[p]], axis=0), jnp.concatenate([bt, bt, kt, kt], axis=0))
        aa.append(jnp.where(aa_mask, a, 0.0))
    x = [a[:c2, :c2] for a in aa]
    z = [jnp.concatenate([ats[p], _bdot(aa[p][:c2, c2:], vs[p])], axis=1) for p in pairs]
    levels = round(math.log(CHUNK, 4))
    assert 4 ** levels == CHUNK
    for lvl in range(levels):
        x2 = [_bdot(x[p], x[p]) for p in pairs]
        if lvl + 1 < levels:
            x34 = [_bdot(x2[p], jnp.concatenate([x[p], x2[p]], axis=1)) for p in pairs]
            x3 = [m[:, :c2] for m in x34]
        else:
            x3 = [_bdot(x2[p], x[p]) for p in pairs]
        z = [z[p] + _bdot(x[p] + x2[p] + x3[p], z[p]) for p in pairs]
        if lvl + 1 < levels:
            x = [m[:, c2:] for m in x34]
    s0 = [s_ref[u] for u in units]
    uv = [jnp.concatenate([(_bdot_nt(z[p][:, :LANES], s0[p]) + z[p][:, LANES:]).astype(BF16),
                           vs[p].astype(BF16)], axis=0) for p in pairs]
    ys = [_bdot_nt(rts[p], s0[p]) + _bdot(aa[p][c2:, :], uv[p]) for p in pairs]
    for p in pairs:
        e_rel = jnp.exp(g_last[p] - g[p])
        s_ref[units[p]] = s0[p] * jnp.exp(g_last[p]) + _bdot_tn(
            uv[p], jnp.concatenate([stack(bb_ref[units[p]] * e_rel), stack(k_ref[units[p]] * e_rel)], axis=0))
    seg_sum = lambda t: jnp.dot(t.astype(BF16), segm, preferred_element_type=F32)
    for n, (bi, p) in enumerate(units):
        y = ys[n][:CHUNK] + ys[n][CHUNK:]
        mean = seg_sum(y) * (1.0 / RWKV_HEAD)
        d = y - mean
        var = seg_sum(d * d) * (1.0 / RWKV_HEAD)
        out = d * lax.rsqrt(var + GN_EPS) * gnw_ref[p] + gnb_ref[p] + bonus_ref[bi, p]
        o_ref[bi, :, p * LANES:(p + 1) * LANES] = out.astype(o_ref.dtype)


def _rwkv_scan(r, g, k, v, kk, bb, bonus, gnw, gnb, segm):
    b, _, s, _ = r.shape
    nb = SCAN_BATCH if b % SCAN_BATCH == 0 else 1
    pm = pl.BlockSpec((nb, N_PAIRS, CHUNK, LANES), lambda bi, c: (bi, 0, c, 0))
    full = lambda a: pl.BlockSpec(a.shape, lambda bi, c: (0,) * a.ndim)
    return pl.pallas_call(
        _rwkv_scan_kernel,
        grid=(b // nb, s // CHUNK),
        in_specs=[pm] * 7 + [full(gnw), full(gnb), full(segm)],
        out_specs=pl.BlockSpec((nb, CHUNK, MIX_WIDTH), lambda bi, c: (bi, c, 0)),
        out_shape=jax.ShapeDtypeStruct((b, s, MIX_WIDTH), BF16),
        scratch_shapes=[pltpu.VMEM((nb, N_PAIRS, LANES, LANES), F32)],
        compiler_params=_cparams(("parallel", "arbitrary")),
        name="rwkv_scan",
    )(r, g, k, v, kk, bb, bonus, gnw, gnb, segm)


def _out_kernel(x_ref, mix_ref, qm_ref, gate_ref, mk_ref, mv_ref, wo_ref, *rest):
    o_ref = rest[-1]
    gate = gate_ref[0].astype(F32)
    sg = gate * (1.0 / (1.0 + jnp.exp(-gate)))
    acc = x_ref[0] + _bdot(mix_ref[0].astype(F32) * sg[:, :MIX_WIDTH], wo_ref[:MIX_WIDTH, :])
    heads = []
    for h in range(X_HEADS):
        sl = slice(h * X_HEAD_DIM, (h + 1) * X_HEAD_DIM)
        s = lax.dot_general(qm_ref[0][:, sl], mk_ref[0][:, sl], (((1,), (1,)), ((), ())),
                            preferred_element_type=F32) * (X_HEAD_DIM ** -0.5)
        p = jnp.exp(s - jnp.max(s, axis=-1, keepdims=True))
        p = p / jnp.sum(p, axis=-1, keepdims=True)
        heads.append(jnp.dot(p.astype(BF16), mv_ref[0][:, sl], preferred_element_type=F32))
    mem_out = jnp.concatenate(heads, axis=1)
    acc = acc + _bdot(mem_out * sg[:, MIX_WIDTH:], wo_ref[MIX_WIDTH:, :])
    if len(rest) == 2:
        ms = jnp.mean(acc * acc, axis=-1, keepdims=True)
        acc = acc * lax.rsqrt(ms + NORM_EPS) * rest[0][...]
    o_ref[0] = acc


def _out(x, mix, qm, gate, mk, mv, wo, ts, final_g=None):
    b, s, d = x.shape
    n_mem = mk.shape[1]
    tok = lambda w: pl.BlockSpec((1, ts, w), lambda bi, i: (bi, i, 0))
    memspec = pl.BlockSpec((1, n_mem, X_WIDTH), lambda bi, i: (bi, 0, 0))
    const = lambda a: pl.BlockSpec(a.shape, lambda bi, i: (0, 0))
    extra = [] if final_g is None else [final_g.reshape(1, d).astype(F32)]
    return pl.pallas_call(
        _out_kernel,
        grid=(b, s // ts),
        in_specs=[tok(d), tok(MIX_WIDTH), tok(X_WIDTH), tok(INNER), memspec, memspec, const(wo)]
                 + [const(a) for a in extra],
        out_specs=tok(d),
        out_shape=jax.ShapeDtypeStruct((b, s, d), F32),
        compiler_params=_cparams(("parallel", "parallel")),
        name="out_proj",
    )(x, mix, qm, gate, mk, mv, wo, *extra)


def _rope_perm():
    return np.concatenate([np.arange(0, QK_ROPE, 2), np.arange(1, QK_ROPE, 2)])


def _pack_mla_weights(w_in, w_uq, w_ukv):
    cq_w, ckv_w, kr_w, qm_w, gate_w = jnp.split(
        w_in, np.cumsum([Q_LORA, KV_LORA, QK_ROPE, X_WIDTH]).tolist(), axis=1)
    perm = _rope_perm()
    kr_w = jnp.pad(kr_w[:, perm], ((0, 0), (0, LANES - QK_ROPE)))
    wq = w_uq.reshape(Q_LORA, MLA_HEADS, QK_NOPE + QK_ROPE)
    wq = jnp.concatenate([wq[:, :, :QK_NOPE], wq[:, :, QK_NOPE:][:, :, perm],
                          jnp.zeros((Q_LORA, MLA_HEADS, QK_PACK - QK_NOPE - QK_ROPE), w_uq.dtype)], axis=2)
    wqt = wq.reshape(Q_LORA, MLA_HEADS * QK_PACK).T
    wkv = w_ukv.reshape(KV_LORA, MLA_HEADS, QK_NOPE + V_HEAD)
    wk = wkv[:, :, :QK_NOPE].reshape(KV_LORA, MLA_HEADS * QK_NOPE)
    wvt = wkv[:, :, QK_NOPE:].reshape(KV_LORA, MLA_HEADS * V_HEAD).T
    bf = lambda a: a.astype(BF16)
    return [bf(cq_w), bf(ckv_w), bf(kr_w), bf(qm_w), bf(gate_w)], bf(wqt), bf(wk), bf(wvt)


def _rope_freqs():
    inv_freq = ROPE_THETA ** (-jnp.arange(0, QK_ROPE, 2, dtype=F32) / QK_ROPE)
    lanes = jnp.concatenate([inv_freq, inv_freq, jnp.zeros((LANES - QK_ROPE,), F32)]).reshape(1, LANES)
    return lanes, inv_freq.reshape(HALF_ROPE, 1)


def _chunk_tri(ts):
    t = np.arange(ts)
    return jnp.asarray((t[:, None] // CHUNK == t[None, :] // CHUNK) & (t[None, :] <= t[:, None]), BF16)


def _head_seg(width):
    t = np.arange(width)
    return jnp.asarray(t[:, None] // RWKV_HEAD == t[None, :] // RWKV_HEAD, BF16)


def kernel(x, mem, positions, norm_g, mem_norm_g, w_mem_kv, w_in_mla, mla_q_norm_g, mla_kv_norm_g, mla_w_uq,
           mla_w_ukv, w_in_rwkv, rwkv_mu, rwkv_w0, rwkv_w2, rwkv_a0, rwkv_a2, rwkv_k_k, rwkv_k_a, rwkv_r_k,
           rwkv_gn_w, rwkv_gn_b, w_out, final_g):
    b, s, d = x.shape
    n_mem = mem.shape[1]
    depth = norm_g.shape[0]
    t = b * s
    tm = min(512, s)
    posc = positions.reshape(b, s, 1).astype(F32)
    posr = positions.reshape(b, 1, s).astype(F32)
    invr, invc = _rope_freqs()
    segm = _head_seg(LANES)
    rwkv_tm = min(RWKV_IN_TILE, s)
    lmat = _chunk_tri(rwkv_tm)
    row = lambda a: a.reshape(1, -1).astype(F32)

    for i in range(depth):
        j = i // 2
        mk, mv = _norm_proj(mem.reshape(b * n_mem, d), mem_norm_g[i],
                            [w_mem_kv[i][:, :X_WIDTH].astype(BF16), w_mem_kv[i][:, X_WIDTH:].astype(BF16)],
                            [BF16, BF16], tm=min(256, b * n_mem))
        mk = mk.reshape(b, n_mem, X_WIDTH)
        mv = mv.reshape(b, n_mem, X_WIDTH)
        if i % 2 == 0:
            w_list, wqt, wk, wvt = _pack_mla_weights(w_in_mla[j], mla_w_uq[j], mla_w_ukv[j])
            cq, ckv, kr, qm, gate = _norm_proj(x.reshape(t, d), norm_g[i], w_list, [F32, F32, F32, BF16, BF16],
                                               tm=tm)
            sh = lambda a: a.reshape(b, s, -1)
            qt, k, vt = _mla_up(sh(cq), sh(ckv), sh(kr), posc, posr, row(mla_q_norm_g[j]), row(mla_kv_norm_g[j]),
                                wqt, wk, wvt, invr, invc, tm=min(MLA_UP_TILE, FLASH_TILE, s),
                                tq=min(FLASH_TILE, s))
            mix = _flash(qt, k, vt, groups=FLASH_GROUPS)
        else:
            w = w_in_rwkv[j]
            edges = np.cumsum([MIX_WIDTH, MIX_WIDTH, MIX_WIDTH, DECAY_LORA + ICLR_LORA, X_WIDTH]).tolist()
            w_list = [a.astype(BF16) for a in jnp.split(w, edges, axis=1)]
            mus = [row(a) for a in jnp.split(rwkv_mu[j], edges[:3])]
            zeros = jnp.zeros((DECAY_LORA, MIX_WIDTH), F32)
            wl = jnp.concatenate([jnp.concatenate([rwkv_w2[j], zeros], axis=1),
                                  jnp.concatenate([zeros, rwkv_a2[j]], axis=1)], axis=0).astype(BF16)
            r, g, kmod, vv, kk, bb, bonus, qm, gate = _rwkv_in(
                x, norm_g[i], w_list, mus, row(rwkv_w0[j]), row(rwkv_a0[j]), wl,
                row(rwkv_k_k[j]), row(rwkv_k_a[j]), row(rwkv_r_k[j]), lmat, _head_seg(SLAB), tm=rwkv_tm)
            gnw = rwkv_gn_w[j].reshape(N_PAIRS, 1, LANES)
            gnb = rwkv_gn_b[j].reshape(N_PAIRS, 1, LANES)
            mix = _rwkv_scan(r, g, kmod, vv, kk, bb, bonus, gnw, gnb, segm)
        x = _out(x, mix, qm.reshape(b, s, -1), gate.reshape(b, s, -1), mk, mv, w_out[i].astype(BF16),
                 ts=min(512, s), final_g=final_g if i == depth - 1 else None)
    return x
```

```python
import functools
import math

import jax
import jax.numpy as jnp
import numpy as np
from jax import lax
from jax.experimental import pallas as pl
from jax.experimental.pallas import tpu as pltpu

F32 = jnp.float32
BF16 = jnp.bfloat16

D_MODEL = 1024
X_HEADS = 4
X_HEAD_DIM = 128
X_WIDTH = X_HEADS * X_HEAD_DIM
MLA_HEADS = 12
QK_NOPE = 128
QK_ROPE = 64
V_HEAD = 128
Q_LORA = 384
KV_LORA = 256
ROPE_THETA = 10000.0
MIX_WIDTH = MLA_HEADS * V_HEAD
INNER = MIX_WIDTH + X_WIDTH
RWKV_HEAD = 64
RWKV_HEADS = MIX_WIDTH // RWKV_HEAD
DECAY_LORA = 64
ICLR_LORA = 64
GN_EPS = 64e-5
NORM_EPS = 1e-6

LANES = 128
QK_PACK = 2 * LANES
N_PAIRS = MIX_WIDTH // LANES
CHUNK = 64
SCAN_BATCH = 2
FLASH_TILE = 2048
FLASH_GROUPS = 8
FLASH_HEADS = 1
MLA_UP_TILE = 512
BF16_SUBLANES = 16
V_AUG = V_HEAD + BF16_SUBLANES
VMEM_LIMIT = 56 * 1024 * 1024


def _cparams(sem):
    return pltpu.CompilerParams(dimension_semantics=sem, vmem_limit_bytes=VMEM_LIMIT)


def _bdot(a, b):
    return jnp.dot(a.astype(BF16), b.astype(BF16), preferred_element_type=F32)


def _bdot_nt(a, b):
    return lax.dot_general(a.astype(BF16), b.astype(BF16), (((1,), (1,)), ((), ())),
                           preferred_element_type=F32)


def _bdot_tn(a, b):
    return lax.dot_general(a.astype(BF16), b.astype(BF16), (((0,), (0,)), ((), ())),
                           preferred_element_type=F32)


def _split_dot(x, m_bf16, terms):
    acc = None
    rem = x
    for _ in range(terms):
        piece = rem.astype(BF16)
        part = jnp.dot(piece, m_bf16, preferred_element_type=F32)
        acc = part if acc is None else acc + part
        rem = rem - piece.astype(F32)
    return acc


def _norm_proj_kernel(x_ref, g_ref, *refs, n_out):
    w_refs, o_refs = refs[:n_out], refs[n_out:]
    x = x_ref[...].astype(F32)
    ms = jnp.mean(x * x, axis=-1, keepdims=True)
    xn = (x * lax.rsqrt(ms + NORM_EPS) * g_ref[...]).astype(BF16)
    for w_ref, o_ref in zip(w_refs, o_refs):
        o_ref[...] = jnp.dot(xn, w_ref[...], preferred_element_type=F32).astype(o_ref.dtype)


def _norm_proj(x2d, g, weights, out_dtypes, tm):
    m, k = x2d.shape
    n_out = len(weights)
    in_specs = [pl.BlockSpec((tm, k), lambda i: (i, 0)), pl.BlockSpec((1, k), lambda i: (0, 0))]
    in_specs += [pl.BlockSpec(w.shape, lambda i: (0, 0)) for w in weights]
    out_specs = [pl.BlockSpec((tm, w.shape[1]), lambda i: (i, 0)) for w in weights]
    out_shape = [jax.ShapeDtypeStruct((m, w.shape[1]), dt) for w, dt in zip(weights, out_dtypes)]
    return pl.pallas_call(
        functools.partial(_norm_proj_kernel, n_out=n_out),
        grid=(m // tm,),
        in_specs=in_specs,
        out_specs=out_specs,
        out_shape=out_shape,
        compiler_params=_cparams(("parallel",)),
        name="norm_proj",
    )(x2d, g.reshape(1, k).astype(F32), *weights)


HALF_ROPE = QK_ROPE // 2


def _mla_up_kernel(cq_ref, ckv_ref, kr_ref, posc_ref, posr_ref, gq_ref, gkv_ref, wqt_ref, wk_ref, wvt_ref,
                   invr_ref, invc_ref, qt_ref, k_ref, vt_ref, *, scale):
    def rms(x, g):
        ms = jnp.mean(x * x, axis=-1, keepdims=True)
        return (x * lax.rsqrt(ms + NORM_EPS) * g).astype(BF16)

    cqn = rms(cq_ref[0], gq_ref[...])
    ckvn = rms(ckv_ref[0], gkv_ref[...])

    ang = posc_ref[0] * invr_ref[...]
    lane = lax.broadcasted_iota(jnp.int32, ang.shape, 1)
    cosv, sinv = jnp.cos(ang), jnp.sin(ang)
    kr = kr_ref[0]
    kr = (kr * cosv
          + pltpu.roll(kr, LANES - HALF_ROPE, 1) * jnp.where(lane < HALF_ROPE, -sinv, 0.0)
          + pltpu.roll(kr, HALF_ROPE, 1) * jnp.where((lane >= HALF_ROPE) & (lane < QK_ROPE), sinv, 0.0))
    kr = kr.astype(BF16)
    kn = jnp.dot(ckvn, wk_ref[...], preferred_element_type=F32).astype(BF16)

    ang_t = invc_ref[...] * posr_ref[0]
    cos_t, sin_t = jnp.cos(ang_t), jnp.sin(ang_t)
    r0, r1, r2 = QK_NOPE, QK_NOPE + HALF_ROPE, QK_NOPE + QK_ROPE
    ones_row = (lax.broadcasted_iota(jnp.int32, (V_AUG - V_HEAD, ang_t.shape[1]), 0) == 0).astype(BF16)
    for h in range(MLA_HEADS):
        k_ref[0, h, :, :LANES] = kn[:, h * LANES:(h + 1) * LANES]
        k_ref[0, h, :, LANES:] = kr
        vt_ref[0, h, 0, :V_HEAD] = _bdot_nt(wvt_ref[h * V_HEAD:(h + 1) * V_HEAD, :], ckvn).astype(BF16)
        vt_ref[0, h, 0, V_HEAD:] = ones_row
        qt = _bdot_nt(wqt_ref[h * QK_PACK:(h + 1) * QK_PACK, :], cqn)
        x1, x2 = qt[r0:r1], qt[r1:r2]
        qt = jnp.concatenate([qt[:r0], x1 * cos_t - x2 * sin_t, x2 * cos_t + x1 * sin_t, qt[r2:]], axis=0)
        qt_ref[0, h, 0] = (qt * scale).astype(BF16)


def _mla_up(cq, ckv, kr, posc, posr, gq, gkv, wqt, wk, wvt, invr, invc, tm, tq):
    b, s, _ = cq.shape
    n, r = s // tq, tq // tm
    tok = lambda w: pl.BlockSpec((1, tm, w), lambda bi, i: (bi, i, 0))
    full = lambda a: pl.BlockSpec(a.shape, lambda bi, i: (0,) * a.ndim)
    scale = float(QK_NOPE + QK_ROPE) ** -0.5 * math.log2(math.e)
    return pl.pallas_call(
        functools.partial(_mla_up_kernel, scale=scale),
        grid=(b, s // tm),
        in_specs=[tok(Q_LORA), tok(KV_LORA), tok(LANES), tok(1),
                  pl.BlockSpec((1, 1, tm), lambda bi, i: (bi, 0, i)),
                  full(gq), full(gkv), full(wqt), full(wk), full(wvt), full(invr), full(invc)],
        out_specs=[pl.BlockSpec((1, MLA_HEADS, 1, QK_PACK, tm), lambda bi, i: (bi, 0, i // r, 0, i % r)),
                   pl.BlockSpec((1, MLA_HEADS, tm, QK_PACK), lambda bi, i: (bi, 0, i, 0)),
                   pl.BlockSpec((1, MLA_HEADS, 1, V_AUG, tm), lambda bi, i: (bi, 0, i // r, 0, i % r))],
        out_shape=[jax.ShapeDtypeStruct((b, MLA_HEADS, n, QK_PACK, tq), BF16),
                   jax.ShapeDtypeStruct((b, MLA_HEADS, s, QK_PACK), BF16),
                   jax.ShapeDtypeStruct((b, MLA_HEADS, n, V_AUG, tq), BF16)],
        compiler_params=_cparams(("parallel", "parallel")),
        name="mla_up",
    )(cq, ckv, kr, posc, posr, gq, gkv, wqt, wk, wvt, invr, invc)


def _flash_kernel(qt_ref, k_ref, vt_ref, o_ref, st_ref, p_ref, acc_ref, *, tq, groups):
    i = pl.program_id(2)
    heads = qt_ref.shape[1]
    gw = tq // groups
    chains = [(hh, g) for hh in range(heads) for g in range(groups)]
    every = tuple(range(len(chains)))
    sel = lambda pred: tuple(n for n in every if pred(chains[n][1]))
    qts = [qt_ref[0, hh, 0, :, g * gw:(g + 1) * gw] for hh, g in chains]

    def scores(tile, d, which):
        mb = {}
        for hh in sorted({chains[n][0] for n in which}):
            kj = k_ref[0, hh, pl.ds(pl.multiple_of(tile * tq, tq) + d * gw, gw), :]
            for g in which:
                if chains[g][0] == hh:
                    st = jnp.dot(kj, qts[g], preferred_element_type=F32)
                    st_ref[d % 2, g] = st
                    mb[g] = jnp.max(st, axis=0, keepdims=True)
        return mb

    def softmax(d, g, mb, m, masked):
        st = st_ref[d % 2, g]
        if masked:
            key = lax.broadcasted_iota(jnp.int32, st.shape, 0)
            qry = lax.broadcasted_iota(jnp.int32, st.shape, 1)
            st = jnp.where(key <= qry, st, -jnp.inf)
            mb = jnp.max(st, axis=0, keepdims=True)
        m_new = jnp.maximum(m, mb)
        p_ref[d % 2, g] = jnp.exp2(st - m_new).astype(BF16)
        return m_new, jnp.exp2(m - m_new)

    def accumulate(tile, d, g, alpha):
        vtj = vt_ref[0, chains[g][0], tile, :, d * gw:(d + 1) * gw]
        acc_ref[g] = alpha * acc_ref[g] + jnp.dot(vtj, p_ref[d % 2, g], preferred_element_type=F32)

    def step(tile, d, state, b_groups, a_groups, c_groups, masked_group):
        mb, m, alpha = state
        nxt = (tile, d + 1) if d + 1 < groups else (tile + 1, 0)
        prv = (tile, d - 1) if d > 0 else (jnp.maximum(tile - 1, 0), groups - 1)
        mb_next = scores(*nxt, a_groups)
        for g in c_groups:
            accumulate(*prv, g, alpha[g])
        m, alpha = list(m), list(alpha)
        for g in b_groups:
            m[g], alpha[g] = softmax(d, g, mb[g], m[g], masked=(chains[g][1] == masked_group))
        return [mb_next.get(g, mb[g]) for g in every], m, alpha

    def trip(u, state):
        for d in range(groups):
            state = step(u, d, state, every, every, every, None)
        return state

    p_ref[(groups - 1) % 2] = jnp.zeros(p_ref.shape[1:], BF16)
    acc_ref[...] = jnp.zeros_like(acc_ref)
    mb0 = scores(0, 0, every)
    per_group = lambda v: [jnp.full((1, gw), v, F32) for _ in every]
    state = ([mb0[g] for g in every], per_group(-jnp.inf), per_group(1.0))
    state = lax.fori_loop(0, i, trip, state)

    for d in range(groups):
        state = step(i, d, state, sel(lambda g: g >= d), sel(lambda g: g > d),
                     every if d == 0 else sel(lambda g: g >= d - 1), d)
    alpha = state[2]
    for n in sel(lambda g: g == groups - 1):
        accumulate(i, groups - 1, n, alpha[n])
    for n, (hh, g) in enumerate(chains):
        acc = acc_ref[n]
        out = acc[:V_HEAD] / acc[V_HEAD:V_HEAD + 1]
        o_ref[0, g * gw:(g + 1) * gw, hh * V_HEAD:(hh + 1) * V_HEAD] = out.T.astype(o_ref.dtype)


def _flash(qt, k, vt, groups, heads):
    b, _, n, _, tq = qt.shape
    s = n * tq
    gw = tq // groups
    return pl.pallas_call(
        functools.partial(_flash_kernel, tq=tq, groups=groups),
        grid=(b, MLA_HEADS // heads, n),
        in_specs=[pl.BlockSpec((1, heads, 1, QK_PACK, tq), lambda bi, h, i: (bi, h, i, 0, 0)),
                  pl.BlockSpec((1, heads, s, QK_PACK), lambda bi, h, i: (bi, h, 0, 0)),
                  pl.BlockSpec((1, heads, n, V_AUG, tq), lambda bi, h, i: (bi, h, 0, 0, 0))],
        out_specs=pl.BlockSpec((1, tq, heads * V_HEAD), lambda bi, h, i: (bi, i, h)),
        out_shape=jax.ShapeDtypeStruct((b, s, MIX_WIDTH), BF16),
        scratch_shapes=[pltpu.VMEM((2, heads * groups, gw, gw), F32),
                        pltpu.VMEM((2, heads * groups, gw, gw), BF16),
                        pltpu.VMEM((heads * groups, V_AUG, gw), F32)],
        compiler_params=_cparams(("parallel", "parallel", "arbitrary")),
        name="flash",
    )(qt, k, vt)


def _split_dot_left(m_bf16, x):
    acc = None
    rem = x
    for _ in range(2):
        piece = rem.astype(BF16)
        part = jnp.dot(m_bf16, piece, preferred_element_type=F32)
        acc = part if acc is None else acc + part
        rem = rem - piece.astype(F32)
    return acc


RWKV_IN_TILE = 256
SLAB = 2 * LANES


def _rwkv_in_kernel(x_ref, g_ref, wr_ref, wk_ref, wv_ref, wwa_ref, wqm_ref, wgate_ref,
                    mur_ref, muk_ref, muv_ref, muwa_ref, w0_ref, a0_ref, wl_ref, kkw_ref, kaw_ref, rkw_ref,
                    lmat_ref, segm_ref,
                    r_o, g_o, k_o, v_o, kk_o, bb_o, bonus_o, qm_o, gate_o,
                    lr_ref, lk_ref, lv_ref, lwa_ref):
    tm = x_ref.shape[1]

    @pl.when(pl.program_id(1) == 0)
    def _():
        for ref in (lr_ref, lk_ref, lv_ref, lwa_ref):
            ref[...] = jnp.zeros_like(ref)

    x = x_ref[0]
    ms = jnp.mean(x * x, axis=-1, keepdims=True)
    xn = (x * lax.rsqrt(ms + NORM_EPS) * g_ref[...]).astype(BF16)
    rid = lax.broadcasted_iota(jnp.int32, (tm, 1), 0)

    def mix(u, last_ref, sl, mu):
        up = jnp.where(rid == 0, last_ref[:, sl], pltpu.roll(u, 1, 0))
        last_ref[:, sl] = u[tm - 1:tm, :]
        return u + (up - u) * mu

    def project(q):
        sl = slice(q * SLAB, (q + 1) * SLAB)
        return tuple(jnp.dot(xn, w_ref[:, sl], preferred_element_type=F32) for w_ref in (wr_ref, wk_ref, wv_ref))

    xwa = mix(jnp.dot(xn, wwa_ref[...], preferred_element_type=F32), lwa_ref, slice(None), muwa_ref[...])
    lane = lax.broadcasted_iota(jnp.int32, xwa.shape, 1)
    z = jnp.where(lane < DECAY_LORA, jnp.tanh(xwa), xwa)
    lo = jnp.dot(z.astype(BF16), wl_ref[...], preferred_element_type=F32)
    lmat = lmat_ref[...]
    segm = segm_ref[...]

    def prepare(q, u):
        sl = slice(q * SLAB, (q + 1) * SLAB)
        xr, xk, xv = (mix(ue, last_ref, sl, mu_ref[:, sl])
                      for ue, last_ref, mu_ref in zip(u, (lr_ref, lk_ref, lv_ref), (mur_ref, muk_ref, muv_ref)))
        lw = -math.exp(-0.5) / (1.0 + jnp.exp(-(w0_ref[:, sl] + lo[:, sl])))
        a = 1.0 / (1.0 + jnp.exp(-(a0_ref[:, sl] + lo[:, MIX_WIDTH + q * SLAB:MIX_WIDTH + (q + 1) * SLAB])))
        kkr = xk * kkw_ref[:, sl]
        kmod = xk * (1.0 + (a - 1.0) * kaw_ref[:, sl])
        n2 = _split_dot(kkr * kkr, segm, 1)
        kk = kkr * lax.rsqrt(jnp.maximum(n2, 1e-24))
        bonus = _split_dot(xr * kmod * rkw_ref[:, sl], segm, 1) * xv
        gcum = _split_dot_left(lmat, lw)
        for half in range(SLAB // LANES):
            p = q * (SLAB // LANES) + half
            hl = slice(half * LANES, (half + 1) * LANES)
            r_o[0, p] = xr[:, hl].astype(r_o.dtype)
            g_o[0, p] = gcum[:, hl]
            k_o[0, p] = kmod[:, hl].astype(k_o.dtype)
            v_o[0, p] = xv[:, hl].astype(v_o.dtype)
            kk_o[0, p] = kk[:, hl].astype(kk_o.dtype)
            bb_o[0, p] = (kk * a)[:, hl].astype(bb_o.dtype)
            bonus_o[0, p] = bonus[:, hl].astype(bonus_o.dtype)

    n_slabs = MIX_WIDTH // SLAB
    pending = project(0)
    for q in range(n_slabs):
        nxt = project(q + 1) if q + 1 < n_slabs else None
        if q + 1 == n_slabs:
            qm_o[0] = jnp.dot(xn, wqm_ref[...], preferred_element_type=F32).astype(qm_o.dtype)
            gate_o[0] = jnp.dot(xn, wgate_ref[...], preferred_element_type=F32).astype(gate_o.dtype)
        prepare(q, pending)
        pending = nxt


def _rwkv_in(x, g, weights, mus, w0, a0, wl, kkw, kaw, rkw, lmat, segm, tm):
    b, s, d = x.shape
    wr, wk, wv, wwa, wqm, wgate = weights
    cur = pl.BlockSpec((1, tm, d), lambda bi, i: (bi, i, 0))
    const = lambda a: pl.BlockSpec(a.shape, lambda bi, i: (0,) * a.ndim, pipeline_mode=pl.Buffered(1))
    pm = pl.BlockSpec((1, N_PAIRS, tm, LANES), lambda bi, i: (bi, 0, i, 0))
    tok = lambda w: pl.BlockSpec((1, tm, w), lambda bi, i: (bi, i, 0))
    small = [g.reshape(1, d).astype(F32), wr, wk, wv, wwa, wqm, wgate, *mus, w0, a0, wl, kkw, kaw, rkw, lmat, segm]
    pair_shape = lambda dt: jax.ShapeDtypeStruct((b, N_PAIRS, s, LANES), dt)
    return pl.pallas_call(
        _rwkv_in_kernel,
        grid=(b, s // tm),
        in_specs=[cur] + [const(a) for a in small],
        out_specs=[pm] * 7 + [tok(X_WIDTH), tok(INNER)],
        out_shape=[pair_shape(F32 if n == 1 else BF16) for n in range(7)]
                  + [jax.ShapeDtypeStruct((b, s, X_WIDTH), BF16), jax.ShapeDtypeStruct((b, s, INNER), BF16)],
        scratch_shapes=[pltpu.VMEM((1, MIX_WIDTH), F32)] * 3 + [pltpu.VMEM((1, LANES), F32)],
        compiler_params=_cparams(("parallel", "arbitrary")),
        name="rwkv_in",
    )(x, *small)


def _rwkv_scan_kernel(r_ref, g_ref, k_ref, v_ref, kk_ref, bb_ref, bonus_ref, gnw_ref, gnb_ref, segm_ref,
                      o_ref, s_ref):
    c2 = 2 * CHUNK

    @pl.when(pl.program_id(1) == 0)
    def _():
        s_ref[...] = jnp.zeros_like(s_ref)

    lane = lax.broadcasted_iota(jnp.int32, (CHUNK, LANES), 1)
    head0 = lane < RWKV_HEAD

    def stack(x):
        return jnp.concatenate([jnp.where(head0, x, 0.0), jnp.where(head0, 0.0, x)], axis=0)

    row = lax.broadcasted_iota(jnp.int32, (2 * c2, 2 * c2), 0)
    col = lax.broadcasted_iota(jnp.int32, (2 * c2, 2 * c2), 1)
    same_head = ((row // CHUNK) % 2) == ((col // CHUNK) % 2)
    tr, tc = row % CHUNK, col % CHUNK
    aa_mask = same_head & ((tc < tr) | ((row >= c2) & (tc == tr)))
    segm = segm_ref[...]

    units = [(bi, p) for bi in range(r_ref.shape[0]) for p in range(N_PAIRS)]
    pairs = range(len(units))
    g = [g_ref[u] for u in units]
    g_last = [gp[CHUNK - 1:CHUNK, :] for gp in g]
    first_row = lax.broadcasted_iota(jnp.int32, (CHUNK, LANES), 0) == 0
    g_prev = [jnp.where(first_row, 0.0, pltpu.roll(gp, 1, 0)) for gp in g]
    ats = [stack(-kk_ref[units[p]] * jnp.exp(g_prev[p])) for p in pairs]
    rts = [stack(r_ref[units[p]] * jnp.exp(g[p])) for p in pairs]
    vs = [stack(v_ref[u]) for u in units]
    aa = []
    for p in pairs:
        e_neg = jnp.exp(-g[p])
        bt, kt = bb_ref[units[p]] * e_neg, k_ref[units[p]] * e_neg
        a = _bdot_nt(jnp.concatenate([ats[p], rts[p]], axis=0), jnp.concatenate([bt, bt, kt, kt], axis=0))
        aa.append(jnp.where(aa_mask, a, 0.0))
    x = [a[:c2, :c2] for a in aa]
    z = [jnp.concatenate([ats[p], _bdot(aa[p][:c2, c2:], vs[p])], axis=1) for p in pairs]
    levels = round(math.log(CHUNK, 4))
    assert 4 ** levels == CHUNK
    for lvl in range(levels):
        x2 = [_bdot(x[p], x[p]) for p in pairs]
        if lvl + 1 < levels:
            x34 = [_bdot(x2[p], jnp.concatenate([x[p], x2[p]], axis=1)) for p in pairs]
            x3 = [m[:, :c2] for m in x34]
        else:
            x3 = [_bdot(x2[p], x[p]) for p in pairs]
        z = [z[p] + _bdot(x[p] + x2[p] + x3[p], z[p]) for p in pairs]
        if lvl + 1 < levels:
            x = [m[:, c2:] for m in x34]
    s0 = [s_ref[u] for u in units]
    uv = [jnp.concatenate([(_bdot_nt(z[p][:, :LANES], s0[p]) + z[p][:, LANES:]).astype(BF16),
                           vs[p].astype(BF16)], axis=0) for p in pairs]
    ys = [_bdot_nt(rts[p], s0[p]) + _bdot(aa[p][c2:, :], uv[p]) for p in pairs]
    for p in pairs:
        e_rel = jnp.exp(g_last[p] - g[p])
        s_ref[units[p]] = s0[p] * jnp.exp(g_last[p]) + _bdot_tn(
            uv[p], jnp.concatenate([stack(bb_ref[units[p]] * e_rel), stack(k_ref[units[p]] * e_rel)], axis=0))
    seg_sum = lambda t: jnp.dot(t.astype(BF16), segm, preferred_element_type=F32)
    for n, (bi, p) in enumerate(units):
        y = ys[n][:CHUNK] + ys[n][CHUNK:]
        mean = seg_sum(y) * (1.0 / RWKV_HEAD)
        d = y - mean
        var = seg_sum(d * d) * (1.0 / RWKV_HEAD)
        out = d * lax.rsqrt(var + GN_EPS) * gnw_ref[p] + gnb_ref[p] + bonus_ref[bi, p]
        o_ref[bi, :, p * LANES:(p + 1) * LANES] = out.astype(o_ref.dtype)


def _rwkv_scan(r, g, k, v, kk, bb, bonus, gnw, gnb, segm):
    b, _, s, _ = r.shape
    nb = SCAN_BATCH if b % SCAN_BATCH == 0 else 1
    pm = pl.BlockSpec((nb, N_PAIRS, CHUNK, LANES), lambda bi, c: (bi, 0, c, 0))
    full = lambda a: pl.BlockSpec(a.shape, lambda bi, c: (0,) * a.ndim)
    return pl.pallas_call(
        _rwkv_scan_kernel,
        grid=(b // nb, s // CHUNK),
        in_specs=[pm] * 7 + [full(gnw), full(gnb), full(segm)],
        out_specs=pl.BlockSpec((nb, CHUNK, MIX_WIDTH), lambda bi, c: (bi, c, 0)),
        out_shape=jax.ShapeDtypeStruct((b, s, MIX_WIDTH), BF16),
        scratch_shapes=[pltpu.VMEM((nb, N_PAIRS, LANES, LANES), F32)],
        compiler_params=_cparams(("parallel", "arbitrary")),
        name="rwkv_scan",
    )(r, g, k, v, kk, bb, bonus, gnw, gnb, segm)


def _out_kernel(x_ref, mix_ref, qm_ref, gate_ref, mk_ref, mvt_ref, wo_ref, *rest):
    o_ref = rest[-1]
    gate = gate_ref[0].astype(F32)
    sg = gate * (1.0 / (1.0 + jnp.exp(-gate)))
    acc = x_ref[0] + _bdot(mix_ref[0].astype(F32) * sg[:, :MIX_WIDTH], wo_ref[:MIX_WIDTH, :])
    hsl = [slice(h * X_HEAD_DIM, (h + 1) * X_HEAD_DIM) for h in range(X_HEADS)]
    q = qm_ref[0]
    st = [lax.dot_general(mk_ref[0][:, sl], q[:, sl], (((1,), (1,)), ((), ())), preferred_element_type=F32)
          * (X_HEAD_DIM ** -0.5) for sl in hsl]
    p = [jnp.exp(s - jnp.max(s, axis=0, keepdims=True)) for s in st]
    ot = [jnp.dot(mvt_ref[0][sl, :], ph.astype(BF16), preferred_element_type=F32) / jnp.sum(ph, axis=0, keepdims=True)
          for sl, ph in zip(hsl, p)]
    mem_out = jnp.concatenate([o.T for o in ot], axis=1)
    acc = acc + _bdot(mem_out * sg[:, MIX_WIDTH:], wo_ref[MIX_WIDTH:, :])
    if len(rest) == 2:
        ms = jnp.mean(acc * acc, axis=-1, keepdims=True)
        acc = acc * lax.rsqrt(ms + NORM_EPS) * rest[0][...]
    o_ref[0] = acc


def _out(x, mix, qm, gate, mk, mvt, wo, ts, final_g=None):
    b, s, d = x.shape
    n_mem = mk.shape[1]
    tok = lambda w: pl.BlockSpec((1, ts, w), lambda bi, i: (bi, i, 0))
    memspec = pl.BlockSpec((1, n_mem, X_WIDTH), lambda bi, i: (bi, 0, 0))
    memtspec = pl.BlockSpec((1, X_WIDTH, n_mem), lambda bi, i: (bi, 0, 0))
    const = lambda a: pl.BlockSpec(a.shape, lambda bi, i: (0, 0))
    extra = [] if final_g is None else [final_g.reshape(1, d).astype(F32)]
    return pl.pallas_call(
        _out_kernel,
        grid=(b, s // ts),
        in_specs=[tok(d), tok(MIX_WIDTH), tok(X_WIDTH), tok(INNER), memspec, memtspec, const(wo)]
                 + [const(a) for a in extra],
        out_specs=tok(d),
        out_shape=jax.ShapeDtypeStruct((b, s, d), F32),
        compiler_params=_cparams(("parallel", "parallel")),
        name="out_proj",
    )(x, mix, qm, gate, mk, mvt, wo, *extra)


def _rope_perm():
    return np.concatenate([np.arange(0, QK_ROPE, 2), np.arange(1, QK_ROPE, 2)])


def _pack_mla_weights(w_in, w_uq, w_ukv):
    cq_w, ckv_w, kr_w, qm_w, gate_w = jnp.split(
        w_in, np.cumsum([Q_LORA, KV_LORA, QK_ROPE, X_WIDTH]).tolist(), axis=1)
    perm = _rope_perm()
    kr_w = jnp.pad(kr_w[:, perm], ((0, 0), (0, LANES - QK_ROPE)))
    wq = w_uq.reshape(Q_LORA, MLA_HEADS, QK_NOPE + QK_ROPE)
    wq = jnp.concatenate([wq[:, :, :QK_NOPE], wq[:, :, QK_NOPE:][:, :, perm],
                          jnp.zeros((Q_LORA, MLA_HEADS, QK_PACK - QK_NOPE - QK_ROPE), w_uq.dtype)], axis=2)
    wqt = wq.reshape(Q_LORA, MLA_HEADS * QK_PACK).T
    wkv = w_ukv.reshape(KV_LORA, MLA_HEADS, QK_NOPE + V_HEAD)
    wk = wkv[:, :, :QK_NOPE].reshape(KV_LORA, MLA_HEADS * QK_NOPE)
    wvt = wkv[:, :, QK_NOPE:].reshape(KV_LORA, MLA_HEADS * V_HEAD).T
    bf = lambda a: a.astype(BF16)
    return [bf(cq_w), bf(ckv_w), bf(kr_w), bf(qm_w), bf(gate_w)], bf(wqt), bf(wk), bf(wvt)


def _rope_freqs():
    inv_freq = ROPE_THETA ** (-jnp.arange(0, QK_ROPE, 2, dtype=F32) / QK_ROPE)
    lanes = jnp.concatenate([inv_freq, inv_freq, jnp.zeros((LANES - QK_ROPE,), F32)]).reshape(1, LANES)
    return lanes, inv_freq.reshape(HALF_ROPE, 1)


def _chunk_tri(ts):
    t = np.arange(ts)
    return jnp.asarray((t[:, None] // CHUNK == t[None, :] // CHUNK) & (t[None, :] <= t[:, None]), BF16)


def _head_seg(width):
    t = np.arange(width)
    return jnp.asarray(t[:, None] // RWKV_HEAD == t[None, :] // RWKV_HEAD, BF16)


def kernel(x, mem, positions, norm_g, mem_norm_g, w_mem_kv, w_in_mla, mla_q_norm_g, mla_kv_norm_g, mla_w_uq,
           mla_w_ukv, w_in_rwkv, rwkv_mu, rwkv_w0, rwkv_w2, rwkv_a0, rwkv_a2, rwkv_k_k, rwkv_k_a, rwkv_r_k,
           rwkv_gn_w, rwkv_gn_b, w_out, final_g):
    b, s, d = x.shape
    n_mem = mem.shape[1]
    depth = norm_g.shape[0]
    t = b * s
    tm = min(512, s)
    posc = positions.reshape(b, s, 1).astype(F32)
    posr = positions.reshape(b, 1, s).astype(F32)
    invr, invc = _rope_freqs()
    segm = _head_seg(LANES)
    rwkv_tm = min(RWKV_IN_TILE, s)
    lmat = _chunk_tri(rwkv_tm)
    row = lambda a: a.reshape(1, -1).astype(F32)

    for i in range(depth):
        j = i // 2
        mk, mv = _norm_proj(mem.reshape(b * n_mem, d), mem_norm_g[i],
                            [w_mem_kv[i][:, :X_WIDTH].astype(BF16), w_mem_kv[i][:, X_WIDTH:].astype(BF16)],
                            [BF16, BF16], tm=min(256, b * n_mem))
        mk = mk.reshape(b, n_mem, X_WIDTH)
        mv = jnp.swapaxes(mv.reshape(b, n_mem, X_WIDTH), 1, 2)
        if i % 2 == 0:
            w_list, wqt, wk, wvt = _pack_mla_weights(w_in_mla[j], mla_w_uq[j], mla_w_ukv[j])
            cq, ckv, kr, qm, gate = _norm_proj(x.reshape(t, d), norm_g[i], w_list, [F32, F32, F32, BF16, BF16],
                                               tm=tm)
            sh = lambda a: a.reshape(b, s, -1)
            qt, k, vt = _mla_up(sh(cq), sh(ckv), sh(kr), posc, posr, row(mla_q_norm_g[j]), row(mla_kv_norm_g[j]),
                                wqt, wk, wvt, invr, invc, tm=min(MLA_UP_TILE, FLASH_TILE, s),
                                tq=min(FLASH_TILE, s))
            mix = _flash(qt, k, vt, groups=FLASH_GROUPS, heads=FLASH_HEADS)
        else:
            w = w_in_rwkv[j]
            edges = np.cumsum([MIX_WIDTH, MIX_WIDTH, MIX_WIDTH, DECAY_LORA + ICLR_LORA, X_WIDTH]).tolist()
            w_list = [a.astype(BF16) for a in jnp.split(w, edges, axis=1)]
            mus = [row(a) for a in jnp.split(rwkv_mu[j], edges[:3])]
            zeros = jnp.zeros((DECAY_LORA, MIX_WIDTH), F32)
            wl = jnp.concatenate([jnp.concatenate([rwkv_w2[j], zeros], axis=1),
                                  jnp.concatenate([zeros, rwkv_a2[j]], axis=1)], axis=0).astype(BF16)
            r, g, kmod, vv, kk, bb, bonus, qm, gate = _rwkv_in(
                x, norm_g[i], w_list, mus, row(rwkv_w0[j]), row(rwkv_a0[j]), wl,
                row(rwkv_k_k[j]), row(rwkv_k_a[j]), row(rwkv_r_k[j]), lmat, _head_seg(SLAB), tm=rwkv_tm)
            gnw = rwkv_gn_w[j].reshape(N_PAIRS, 1, LANES)
            gnb = rwkv_gn_b[j].reshape(N_PAIRS, 1, LANES)
            mix = _rwkv_scan(r, g, kmod, vv, kk, bb, bonus, gnw, gnb, segm)
        x = _out(x, mix, qm.reshape(b, s, -1), gate.reshape(b, s, -1), mk, mv, w_out[i].astype(BF16),
                 ts=min(512, s), final_g=final_g if i == depth - 1 else None)
    return x
```

```python
import functools
import math

import jax
import jax.numpy as jnp
import numpy as np
from jax import lax
from jax.experimental import pallas as pl
from jax.experimental.pallas import tpu as pltpu

F32 = jnp.float32
BF16 = jnp.bfloat16

D_MODEL = 1024
X_HEADS = 4
X_HEAD_DIM = 128
X_WIDTH = X_HEADS * X_HEAD_DIM
MLA_HEADS = 12
QK_NOPE = 128
QK_ROPE = 64
V_HEAD = 128
Q_LORA = 384
KV_LORA = 256
ROPE_THETA = 10000.0
MIX_WIDTH = MLA_HEADS * V_HEAD
INNER = MIX_WIDTH + X_WIDTH
RWKV_HEAD = 64
RWKV_HEADS = MIX_WIDTH // RWKV_HEAD
DECAY_LORA = 64
ICLR_LORA = 64
GN_EPS = 64e-5
NORM_EPS = 1e-6

LANES = 128
QK_PACK = 2 * LANES
N_PAIRS = MIX_WIDTH // LANES
CHUNK = 64
SCAN_BATCH = 4
FLASH_TILE = 2048
FLASH_GROUPS = 8
FLASH_HEADS = 1
MLA_UP_TILE = 512
BF16_SUBLANES = 16
V_AUG = V_HEAD + BF16_SUBLANES
VMEM_LIMIT = 56 * 1024 * 1024


def _cparams(sem):
    return pltpu.CompilerParams(dimension_semantics=sem, vmem_limit_bytes=VMEM_LIMIT)


def _bdot(a, b):
    return jnp.dot(a.astype(BF16), b.astype(BF16), preferred_element_type=F32)


def _bdot_nt(a, b):
    return lax.dot_general(a.astype(BF16), b.astype(BF16), (((1,), (1,)), ((), ())),
                           preferred_element_type=F32)


def _bdot_tn(a, b):
    return lax.dot_general(a.astype(BF16), b.astype(BF16), (((0,), (0,)), ((), ())),
                           preferred_element_type=F32)


def _split_dot(x, m_bf16, terms):
    acc = None
    rem = x
    for _ in range(terms):
        piece = rem.astype(BF16)
        part = jnp.dot(piece, m_bf16, preferred_element_type=F32)
        acc = part if acc is None else acc + part
        rem = rem - piece.astype(F32)
    return acc


def _norm_proj_kernel(x_ref, g_ref, *refs, n_out):
    w_refs, o_refs = refs[:n_out], refs[n_out:]
    x = x_ref[...].astype(F32)
    ms = jnp.mean(x * x, axis=-1, keepdims=True)
    xn = (x * lax.rsqrt(ms + NORM_EPS) * g_ref[...]).astype(BF16)
    for w_ref, o_ref in zip(w_refs, o_refs):
        o_ref[...] = jnp.dot(xn, w_ref[...], preferred_element_type=F32).astype(o_ref.dtype)


def _norm_proj(x2d, g, weights, out_dtypes, tm):
    m, k = x2d.shape
    n_out = len(weights)
    in_specs = [pl.BlockSpec((tm, k), lambda i: (i, 0)), pl.BlockSpec((1, k), lambda i: (0, 0))]
    in_specs += [pl.BlockSpec(w.shape, lambda i: (0, 0)) for w in weights]
    out_specs = [pl.BlockSpec((tm, w.shape[1]), lambda i: (i, 0)) for w in weights]
    out_shape = [jax.ShapeDtypeStruct((m, w.shape[1]), dt) for w, dt in zip(weights, out_dtypes)]
    return pl.pallas_call(
        functools.partial(_norm_proj_kernel, n_out=n_out),
        grid=(m // tm,),
        in_specs=in_specs,
        out_specs=out_specs,
        out_shape=out_shape,
        compiler_params=_cparams(("parallel",)),
        name="norm_proj",
    )(x2d, g.reshape(1, k).astype(F32), *weights)


HALF_ROPE = QK_ROPE // 2


def _mla_up_kernel(cq_ref, ckv_ref, kr_ref, posc_ref, posr_ref, gq_ref, gkv_ref, wqt_ref, wk_ref, wvt_ref,
                   invr_ref, invc_ref, qt_ref, k_ref, vt_ref, *, scale):
    def rms(x, g):
        ms = jnp.mean(x * x, axis=-1, keepdims=True)
        return (x * lax.rsqrt(ms + NORM_EPS) * g).astype(BF16)

    cqn = rms(cq_ref[0], gq_ref[...])
    ckvn = rms(ckv_ref[0], gkv_ref[...])

    ang = posc_ref[0] * invr_ref[...]
    lane = lax.broadcasted_iota(jnp.int32, ang.shape, 1)
    cosv, sinv = jnp.cos(ang), jnp.sin(ang)
    kr = kr_ref[0]
    kr = (kr * cosv
          + pltpu.roll(kr, LANES - HALF_ROPE, 1) * jnp.where(lane < HALF_ROPE, -sinv, 0.0)
          + pltpu.roll(kr, HALF_ROPE, 1) * jnp.where((lane >= HALF_ROPE) & (lane < QK_ROPE), sinv, 0.0))
    kr = kr.astype(BF16)
    kn = jnp.dot(ckvn, wk_ref[...], preferred_element_type=F32).astype(BF16)

    ang_t = invc_ref[...] * posr_ref[0]
    cos_t, sin_t = jnp.cos(ang_t), jnp.sin(ang_t)
    r0, r1, r2 = QK_NOPE, QK_NOPE + HALF_ROPE, QK_NOPE + QK_ROPE
    ones_row = (lax.broadcasted_iota(jnp.int32, (V_AUG - V_HEAD, ang_t.shape[1]), 0) == 0).astype(BF16)
    for h in range(MLA_HEADS):
        k_ref[0, h, :, :LANES] = kn[:, h * LANES:(h + 1) * LANES]
        k_ref[0, h, :, LANES:] = kr
        vt_ref[0, h, 0, :V_HEAD] = _bdot_nt(wvt_ref[h * V_HEAD:(h + 1) * V_HEAD, :], ckvn).astype(BF16)
        vt_ref[0, h, 0, V_HEAD:] = ones_row
        qt = _bdot_nt(wqt_ref[h * QK_PACK:(h + 1) * QK_PACK, :], cqn)
        x1, x2 = qt[r0:r1], qt[r1:r2]
        qt = jnp.concatenate([qt[:r0], x1 * cos_t - x2 * sin_t, x2 * cos_t + x1 * sin_t, qt[r2:]], axis=0)
        qt_ref[0, h, 0] = (qt * scale).astype(BF16)


def _mla_up(cq, ckv, kr, posc, posr, gq, gkv, wqt, wk, wvt, invr, invc, tm, tq):
    b, s, _ = cq.shape
    n, r = s // tq, tq // tm
    tok = lambda w: pl.BlockSpec((1, tm, w), lambda bi, i: (bi, i, 0))
    full = lambda a: pl.BlockSpec(a.shape, lambda bi, i: (0,) * a.ndim)
    scale = float(QK_NOPE + QK_ROPE) ** -0.5 * math.log2(math.e)
    return pl.pallas_call(
        functools.partial(_mla_up_kernel, scale=scale),
        grid=(b, s // tm),
        in_specs=[tok(Q_LORA), tok(KV_LORA), tok(LANES), tok(1),
                  pl.BlockSpec((1, 1, tm), lambda bi, i: (bi, 0, i)),
                  full(gq), full(gkv), full(wqt), full(wk), full(wvt), full(invr), full(invc)],
        out_specs=[pl.BlockSpec((1, MLA_HEADS, 1, QK_PACK, tm), lambda bi, i: (bi, 0, i // r, 0, i % r)),
                   pl.BlockSpec((1, MLA_HEADS, tm, QK_PACK), lambda bi, i: (bi, 0, i, 0)),
                   pl.BlockSpec((1, MLA_HEADS, 1, V_AUG, tm), lambda bi, i: (bi, 0, i // r, 0, i % r))],
        out_shape=[jax.ShapeDtypeStruct((b, MLA_HEADS, n, QK_PACK, tq), BF16),
                   jax.ShapeDtypeStruct((b, MLA_HEADS, s, QK_PACK), BF16),
                   jax.ShapeDtypeStruct((b, MLA_HEADS, n, V_AUG, tq), BF16)],
        compiler_params=_cparams(("parallel", "parallel")),
        name="mla_up",
    )(cq, ckv, kr, posc, posr, gq, gkv, wqt, wk, wvt, invr, invc)


def _flash_kernel(qt_ref, k_ref, vt_ref, o_ref, st_ref, p_ref, acc_ref, *, tq, groups):
    i = pl.program_id(2)
    heads = qt_ref.shape[1]
    gw = tq // groups
    chains = [(hh, g) for hh in range(heads) for g in range(groups)]
    every = tuple(range(len(chains)))
    sel = lambda pred: tuple(n for n in every if pred(chains[n][1]))
    qts = [qt_ref[0, hh, 0, :, g * gw:(g + 1) * gw] for hh, g in chains]

    def scores(tile, d, which):
        mb = {}
        for hh in sorted({chains[n][0] for n in which}):
            kj = k_ref[0, hh, pl.ds(pl.multiple_of(tile * tq, tq) + d * gw, gw), :]
            for g in which:
                if chains[g][0] == hh:
                    st = jnp.dot(kj, qts[g], preferred_element_type=F32)
                    st_ref[d % 2, g] = st
                    mb[g] = jnp.max(st, axis=0, keepdims=True)
        return mb

    def softmax(d, g, mb, m, masked):
        st = st_ref[d % 2, g]
        if masked:
            key = lax.broadcasted_iota(jnp.int32, st.shape, 0)
            qry = lax.broadcasted_iota(jnp.int32, st.shape, 1)
            st = jnp.where(key <= qry, st, -jnp.inf)
            mb = jnp.max(st, axis=0, keepdims=True)
        m_new = jnp.maximum(m, mb)
        p_ref[d % 2, g] = jnp.exp2(st - m_new).astype(BF16)
        return m_new, jnp.exp2(m - m_new)

    def accumulate(tile, d, g, alpha):
        vtj = vt_ref[0, chains[g][0], tile, :, d * gw:(d + 1) * gw]
        acc_ref[g] = alpha * acc_ref[g] + jnp.dot(vtj, p_ref[d % 2, g], preferred_element_type=F32)

    def step(tile, d, state, b_groups, a_groups, c_groups, masked_group):
        mb, m, alpha = state
        nxt = (tile, d + 1) if d + 1 < groups else (tile + 1, 0)
        prv = (tile, d - 1) if d > 0 else (jnp.maximum(tile - 1, 0), groups - 1)
        mb_next = scores(*nxt, a_groups)
        for g in c_groups:
            accumulate(*prv, g, alpha[g])
        m, alpha = list(m), list(alpha)
        for g in b_groups:
            m[g], alpha[g] = softmax(d, g, mb[g], m[g], masked=(chains[g][1] == masked_group))
        return [mb_next.get(g, mb[g]) for g in every], m, alpha

    def trip(u, state):
        for d in range(groups):
            state = step(u, d, state, every, every, every, None)
        return state

    p_ref[(groups - 1) % 2] = jnp.zeros(p_ref.shape[1:], BF16)
    acc_ref[...] = jnp.zeros_like(acc_ref)
    mb0 = scores(0, 0, every)
    per_group = lambda v: [jnp.full((1, gw), v, F32) for _ in every]
    state = ([mb0[g] for g in every], per_group(-jnp.inf), per_group(1.0))
    state = lax.fori_loop(0, i, trip, state)

    for d in range(groups):
        state = step(i, d, state, sel(lambda g: g >= d), sel(lambda g: g > d),
                     every if d == 0 else sel(lambda g: g >= d - 1), d)
    alpha = state[2]
    for n in sel(lambda g: g == groups - 1):
        accumulate(i, groups - 1, n, alpha[n])
    for n, (hh, g) in enumerate(chains):
        acc = acc_ref[n]
        out = acc[:V_HEAD] / acc[V_HEAD:V_HEAD + 1]
        o_ref[0, g * gw:(g + 1) * gw, hh * V_HEAD:(hh + 1) * V_HEAD] = out.T.astype(o_ref.dtype)


def _flash(qt, k, vt, groups, heads):
    b, _, n, _, tq = qt.shape
    s = n * tq
    gw = tq // groups
    return pl.pallas_call(
        functools.partial(_flash_kernel, tq=tq, groups=groups),
        grid=(b, MLA_HEADS // heads, n),
        in_specs=[pl.BlockSpec((1, heads, 1, QK_PACK, tq), lambda bi, h, i: (bi, h, i, 0, 0)),
                  pl.BlockSpec((1, heads, s, QK_PACK), lambda bi, h, i: (bi, h, 0, 0)),
                  pl.BlockSpec((1, heads, n, V_AUG, tq), lambda bi, h, i: (bi, h, 0, 0, 0))],
        out_specs=pl.BlockSpec((1, tq, heads * V_HEAD), lambda bi, h, i: (bi, i, h)),
        out_shape=jax.ShapeDtypeStruct((b, s, MIX_WIDTH), BF16),
        scratch_shapes=[pltpu.VMEM((2, heads * groups, gw, gw), F32),
                        pltpu.VMEM((2, heads * groups, gw, gw), BF16),
                        pltpu.VMEM((heads * groups, V_AUG, gw), F32)],
        compiler_params=_cparams(("parallel", "parallel", "arbitrary")),
        name="flash",
    )(qt, k, vt)


def _split_dot_left(m_bf16, x):
    acc = None
    rem = x
    for _ in range(2):
        piece = rem.astype(BF16)
        part = jnp.dot(m_bf16, piece, preferred_element_type=F32)
        acc = part if acc is None else acc + part
        rem = rem - piece.astype(F32)
    return acc


RWKV_IN_TILE = 256
SLAB = 2 * LANES


def _rwkv_in_kernel(x_ref, g_ref, wr_ref, wk_ref, wv_ref, wwa_ref, wqm_ref, wgate_ref,
                    mur_ref, muk_ref, muv_ref, muwa_ref, w0_ref, a0_ref, wl_ref, kkw_ref, kaw_ref, rkw_ref,
                    lmat_ref, segm_ref,
                    r_o, g_o, k_o, v_o, kk_o, bb_o, bonus_o, qm_o, gate_o,
                    lr_ref, lk_ref, lv_ref, lwa_ref):
    tm = x_ref.shape[1]

    @pl.when(pl.program_id(1) == 0)
    def _():
        for ref in (lr_ref, lk_ref, lv_ref, lwa_ref):
            ref[...] = jnp.zeros_like(ref)

    x = x_ref[0]
    ms = jnp.mean(x * x, axis=-1, keepdims=True)
    xn = (x * lax.rsqrt(ms + NORM_EPS) * g_ref[...]).astype(BF16)
    rid = lax.broadcasted_iota(jnp.int32, (tm, 1), 0)

    def mix(u, last_ref, sl, mu):
        up = jnp.where(rid == 0, last_ref[:, sl], pltpu.roll(u, 1, 0))
        last_ref[:, sl] = u[tm - 1:tm, :]
        return u + (up - u) * mu

    def project(q):
        sl = slice(q * SLAB, (q + 1) * SLAB)
        return tuple(jnp.dot(xn, w_ref[:, sl], preferred_element_type=F32) for w_ref in (wr_ref, wk_ref, wv_ref))

    xwa = mix(jnp.dot(xn, wwa_ref[...], preferred_element_type=F32), lwa_ref, slice(None), muwa_ref[...])
    lane = lax.broadcasted_iota(jnp.int32, xwa.shape, 1)
    z = jnp.where(lane < DECAY_LORA, jnp.tanh(xwa), xwa)
    lo = jnp.dot(z.astype(BF16), wl_ref[...], preferred_element_type=F32)
    lmat = lmat_ref[...]
    segm = segm_ref[...]

    def prepare(q, u):
        sl = slice(q * SLAB, (q + 1) * SLAB)
        xr, xk, xv = (mix(ue, last_ref, sl, mu_ref[:, sl])
                      for ue, last_ref, mu_ref in zip(u, (lr_ref, lk_ref, lv_ref), (mur_ref, muk_ref, muv_ref)))
        lw = -math.exp(-0.5) / (1.0 + jnp.exp(-(w0_ref[:, sl] + lo[:, sl])))
        a = 1.0 / (1.0 + jnp.exp(-(a0_ref[:, sl] + lo[:, MIX_WIDTH + q * SLAB:MIX_WIDTH + (q + 1) * SLAB])))
        kkr = xk * kkw_ref[:, sl]
        kmod = xk * (1.0 + (a - 1.0) * kaw_ref[:, sl])
        n2 = _split_dot(kkr * kkr, segm, 1)
        kk = kkr * lax.rsqrt(jnp.maximum(n2, 1e-24))
        bonus = _split_dot(xr * kmod * rkw_ref[:, sl], segm, 1) * xv
        gcum = _split_dot_left(lmat, lw)
        for half in range(SLAB // LANES):
            p = q * (SLAB // LANES) + half
            hl = slice(half * LANES, (half + 1) * LANES)
            r_o[0, p] = xr[:, hl].astype(r_o.dtype)
            g_o[0, p] = gcum[:, hl]
            k_o[0, p] = kmod[:, hl].astype(k_o.dtype)
            v_o[0, p] = xv[:, hl].astype(v_o.dtype)
            kk_o[0, p] = kk[:, hl].astype(kk_o.dtype)
            bb_o[0, p] = (kk * a)[:, hl].astype(bb_o.dtype)
            bonus_o[0, p] = bonus[:, hl].astype(bonus_o.dtype)

    n_slabs = MIX_WIDTH // SLAB
    pending = project(0)
    for q in range(n_slabs):
        nxt = project(q + 1) if q + 1 < n_slabs else None
        if q + 1 == n_slabs:
            qm_o[0] = jnp.dot(xn, wqm_ref[...], preferred_element_type=F32).astype(qm_o.dtype)
            gate_o[0] = jnp.dot(xn, wgate_ref[...], preferred_element_type=F32).astype(gate_o.dtype)
        prepare(q, pending)
        pending = nxt


def _rwkv_in(x, g, weights, mus, w0, a0, wl, kkw, kaw, rkw, lmat, segm, tm):
    b, s, d = x.shape
    wr, wk, wv, wwa, wqm, wgate = weights
    cur = pl.BlockSpec((1, tm, d), lambda bi, i: (bi, i, 0))
    const = lambda a: pl.BlockSpec(a.shape, lambda bi, i: (0,) * a.ndim, pipeline_mode=pl.Buffered(1))
    pm = pl.BlockSpec((1, N_PAIRS, tm, LANES), lambda bi, i: (bi, 0, i, 0))
    tok = lambda w: pl.BlockSpec((1, tm, w), lambda bi, i: (bi, i, 0))
    small = [g.reshape(1, d).astype(F32), wr, wk, wv, wwa, wqm, wgate, *mus, w0, a0, wl, kkw, kaw, rkw, lmat, segm]
    pair_shape = lambda dt: jax.ShapeDtypeStruct((b, N_PAIRS, s, LANES), dt)
    return pl.pallas_call(
        _rwkv_in_kernel,
        grid=(b, s // tm),
        in_specs=[cur] + [const(a) for a in small],
        out_specs=[pm] * 7 + [tok(X_WIDTH), tok(INNER)],
        out_shape=[pair_shape(F32 if n == 1 else BF16) for n in range(7)]
                  + [jax.ShapeDtypeStruct((b, s, X_WIDTH), BF16), jax.ShapeDtypeStruct((b, s, INNER), BF16)],
        scratch_shapes=[pltpu.VMEM((1, MIX_WIDTH), F32)] * 3 + [pltpu.VMEM((1, LANES), F32)],
        compiler_params=_cparams(("parallel", "arbitrary")),
        name="rwkv_in",
    )(x, *small)


def _rwkv_scan_kernel(r_ref, g_ref, k_ref, v_ref, kk_ref, bb_ref, bonus_ref, gnw_ref, gnb_ref, segm_ref,
                      o_ref, s_ref):
    c2 = 2 * CHUNK

    @pl.when(pl.program_id(1) == 0)
    def _():
        s_ref[...] = jnp.zeros_like(s_ref)

    lane = lax.broadcasted_iota(jnp.int32, (CHUNK, LANES), 1)
    head0 = lane < RWKV_HEAD

    def stack(x):
        return jnp.concatenate([jnp.where(head0, x, 0.0), jnp.where(head0, 0.0, x)], axis=0)

    row = lax.broadcasted_iota(jnp.int32, (2 * c2, 2 * c2), 0)
    col = lax.broadcasted_iota(jnp.int32, (2 * c2, 2 * c2), 1)
    same_head = ((row // CHUNK) % 2) == ((col // CHUNK) % 2)
    tr, tc = row % CHUNK, col % CHUNK
    aa_mask = same_head & ((tc < tr) | ((row >= c2) & (tc == tr)))
    segm = segm_ref[...]

    units = [(bi, p) for bi in range(r_ref.shape[0]) for p in range(N_PAIRS)]
    pairs = range(len(units))
    g = [g_ref[u] for u in units]
    g_last = [gp[CHUNK - 1:CHUNK, :] for gp in g]
    first_row = lax.broadcasted_iota(jnp.int32, (CHUNK, LANES), 0) == 0
    g_prev = [jnp.where(first_row, 0.0, pltpu.roll(gp, 1, 0)) for gp in g]
    ats = [stack(-kk_ref[units[p]] * jnp.exp(g_prev[p])) for p in pairs]
    rts = [stack(r_ref[units[p]] * jnp.exp(g[p])) for p in pairs]
    vs = [stack(v_ref[u]) for u in units]
    aa = []
    for p in pairs:
        e_neg = jnp.exp(-g[p])
        bt, kt = bb_ref[units[p]] * e_neg, k_ref[units[p]] * e_neg
        a = _bdot_nt(jnp.concatenate([ats[p], rts[p]], axis=0), jnp.concatenate([bt, bt, kt, kt], axis=0))
        aa.append(jnp.where(aa_mask, a, 0.0))
    x = [a[:c2, :c2] for a in aa]
    z = [jnp.concatenate([ats[p], _bdot(aa[p][:c2, c2:], vs[p])], axis=1) for p in pairs]
    levels = round(math.log(CHUNK, 4))
    assert 4 ** levels == CHUNK
    for lvl in range(levels):
        x2 = [_bdot(x[p], x[p]) for p in pairs]
        if lvl + 1 < levels:
            x34 = [_bdot(x2[p], jnp.concatenate([x[p], x2[p]], axis=1)) for p in pairs]
            x3 = [m[:, :c2] for m in x34]
        else:
            x3 = [_bdot(x2[p], x[p]) for p in pairs]
        z = [z[p] + _bdot(x[p] + x2[p] + x3[p], z[p]) for p in pairs]
        if lvl + 1 < levels:
            x = [m[:, c2:] for m in x34]
    s0 = [s_ref[u] for u in units]
    uv = [jnp.concatenate([(_bdot_nt(z[p][:, :LANES], s0[p]) + z[p][:, LANES:]).astype(BF16),
                           vs[p].astype(BF16)], axis=0) for p in pairs]
    ys = [_bdot_nt(rts[p], s0[p]) + _bdot(aa[p][c2:, :], uv[p]) for p in pairs]
    for p in pairs:
        e_rel = jnp.exp(g_last[p] - g[p])
        s_ref[units[p]] = s0[p] * jnp.exp(g_last[p]) + _bdot_tn(
            uv[p], jnp.concatenate([stack(bb_ref[units[p]] * e_rel), stack(k_ref[units[p]] * e_rel)], axis=0))
    seg_sum = lambda t: jnp.dot(t.astype(BF16), segm, preferred_element_type=F32)
    for n, (bi, p) in enumerate(units):
        y = ys[n][:CHUNK] + ys[n][CHUNK:]
        mean = seg_sum(y) * (1.0 / RWKV_HEAD)
        d = y - mean
        var = seg_sum(d * d) * (1.0 / RWKV_HEAD)
        out = d * lax.rsqrt(var + GN_EPS) * gnw_ref[p] + gnb_ref[p] + bonus_ref[bi, p]
        o_ref[bi, :, p * LANES:(p + 1) * LANES] = out.astype(o_ref.dtype)


def _rwkv_scan(r, g, k, v, kk, bb, bonus, gnw, gnb, segm):
    b, _, s, _ = r.shape
    nb = SCAN_BATCH if b % SCAN_BATCH == 0 else 1
    pm = pl.BlockSpec((nb, N_PAIRS, CHUNK, LANES), lambda bi, c: (bi, 0, c, 0))
    full = lambda a: pl.BlockSpec(a.shape, lambda bi, c: (0,) * a.ndim)
    return pl.pallas_call(
        _rwkv_scan_kernel,
        grid=(b // nb, s // CHUNK),
        in_specs=[pm] * 7 + [full(gnw), full(gnb), full(segm)],
        out_specs=pl.BlockSpec((nb, CHUNK, MIX_WIDTH), lambda bi, c: (bi, c, 0)),
        out_shape=jax.ShapeDtypeStruct((b, s, MIX_WIDTH), BF16),
        scratch_shapes=[pltpu.VMEM((nb, N_PAIRS, LANES, LANES), F32)],
        compiler_params=_cparams(("parallel", "arbitrary")),
        name="rwkv_scan",
    )(r, g, k, v, kk, bb, bonus, gnw, gnb, segm)


def _out_kernel(x_ref, mix_ref, qm_ref, gate_ref, mk_ref, mvt_ref, wo_ref, *rest):
    o_ref = rest[-1]
    gate = gate_ref[0].astype(F32)
    sg = gate * (1.0 / (1.0 + jnp.exp(-gate)))
    acc = x_ref[0] + _bdot(mix_ref[0].astype(F32) * sg[:, :MIX_WIDTH], wo_ref[:MIX_WIDTH, :])
    hsl = [slice(h * X_HEAD_DIM, (h + 1) * X_HEAD_DIM) for h in range(X_HEADS)]
    q = qm_ref[0]
    st = [lax.dot_general(mk_ref[0][:, sl], q[:, sl], (((1,), (1,)), ((), ())), preferred_element_type=F32)
          * (X_HEAD_DIM ** -0.5) for sl in hsl]
    p = [jnp.exp(s - jnp.max(s, axis=0, keepdims=True)) for s in st]
    ot = [jnp.dot(mvt_ref[0][sl, :], ph.astype(BF16), preferred_element_type=F32) / jnp.sum(ph, axis=0, keepdims=True)
          for sl, ph in zip(hsl, p)]
    mem_out = jnp.concatenate([o.T for o in ot], axis=1)
    acc = acc + _bdot(mem_out * sg[:, MIX_WIDTH:], wo_ref[MIX_WIDTH:, :])
    if len(rest) == 2:
        ms = jnp.mean(acc * acc, axis=-1, keepdims=True)
        acc = acc * lax.rsqrt(ms + NORM_EPS) * rest[0][...]
    o_ref[0] = acc


def _out(x, mix, qm, gate, mk, mvt, wo, ts, final_g=None):
    b, s, d = x.shape
    n_mem = mk.shape[1]
    tok = lambda w: pl.BlockSpec((1, ts, w), lambda bi, i: (bi, i, 0))
    memspec = pl.BlockSpec((1, n_mem, X_WIDTH), lambda bi, i: (bi, 0, 0))
    memtspec = pl.BlockSpec((1, X_WIDTH, n_mem), lambda bi, i: (bi, 0, 0))
    const = lambda a: pl.BlockSpec(a.shape, lambda bi, i: (0, 0))
    extra = [] if final_g is None else [final_g.reshape(1, d).astype(F32)]
    return pl.pallas_call(
        _out_kernel,
        grid=(b, s // ts),
        in_specs=[tok(d), tok(MIX_WIDTH), tok(X_WIDTH), tok(INNER), memspec, memtspec, const(wo)]
                 + [const(a) for a in extra],
        out_specs=tok(d),
        out_shape=jax.ShapeDtypeStruct((b, s, d), F32),
        compiler_params=_cparams(("parallel", "parallel")),
        name="out_proj",
    )(x, mix, qm, gate, mk, mvt, wo, *extra)


def _rope_perm():
    return np.concatenate([np.arange(0, QK_ROPE, 2), np.arange(1, QK_ROPE, 2)])


def _pack_mla_weights(w_in, w_uq, w_ukv):
    cq_w, ckv_w, kr_w, qm_w, gate_w = jnp.split(
        w_in, np.cumsum([Q_LORA, KV_LORA, QK_ROPE, X_WIDTH]).tolist(), axis=1)
    perm = _rope_perm()
    kr_w = jnp.pad(kr_w[:, perm], ((0, 0), (0, LANES - QK_ROPE)))
    wq = w_uq.reshape(Q_LORA, MLA_HEADS, QK_NOPE + QK_ROPE)
    wq = jnp.concatenate([wq[:, :, :QK_NOPE], wq[:, :, QK_NOPE:][:, :, perm],
                          jnp.zeros((Q_LORA, MLA_HEADS, QK_PACK - QK_NOPE - QK_ROPE), w_uq.dtype)], axis=2)
    wqt = wq.reshape(Q_LORA, MLA_HEADS * QK_PACK).T
    wkv = w_ukv.reshape(KV_LORA, MLA_HEADS, QK_NOPE + V_HEAD)
    wk = wkv[:, :, :QK_NOPE].reshape(KV_LORA, MLA_HEADS * QK_NOPE)
    wvt = wkv[:, :, QK_NOPE:].reshape(KV_LORA, MLA_HEADS * V_HEAD).T
    bf = lambda a: a.astype(BF16)
    return [bf(cq_w), bf(ckv_w), bf(kr_w), bf(qm_w), bf(gate_w)], bf(wqt), bf(wk), bf(wvt)


def _rope_freqs():
    inv_freq = ROPE_THETA ** (-jnp.arange(0, QK_ROPE, 2, dtype=F32) / QK_ROPE)
    lanes = jnp.concatenate([inv_freq, inv_freq, jnp.zeros((LANES - QK_ROPE,), F32)]).reshape(1, LANES)
    return lanes, inv_freq.reshape(HALF_ROPE, 1)


def _chunk_tri(ts):
    t = np.arange(ts)
    return jnp.asarray((t[:, None] // CHUNK == t[None, :] // CHUNK) & (t[None, :] <= t[:, None]), BF16)


def _head_seg(width):
    t = np.arange(width)
    return jnp.asarray(t[:, None] // RWKV_HEAD == t[None, :] // RWKV_HEAD, BF16)


def kernel(x, mem, positions, norm_g, mem_norm_g, w_mem_kv, w_in_mla, mla_q_norm_g, mla_kv_norm_g, mla_w_uq,
           mla_w_ukv, w_in_rwkv, rwkv_mu, rwkv_w0, rwkv_w2, rwkv_a0, rwkv_a2, rwkv_k_k, rwkv_k_a, rwkv_r_k,
           rwkv_gn_w, rwkv_gn_b, w_out, final_g):
    b, s, d = x.shape
    n_mem = mem.shape[1]
    depth = norm_g.shape[0]
    t = b * s
    tm = min(512, s)
    posc = positions.reshape(b, s, 1).astype(F32)
    posr = positions.reshape(b, 1, s).astype(F32)
    invr, invc = _rope_freqs()
    segm = _head_seg(LANES)
    rwkv_tm = min(RWKV_IN_TILE, s)
    lmat = _chunk_tri(rwkv_tm)
    row = lambda a: a.reshape(1, -1).astype(F32)

    for i in range(depth):
        j = i // 2
        mk, mv = _norm_proj(mem.reshape(b * n_mem, d), mem_norm_g[i],
                            [w_mem_kv[i][:, :X_WIDTH].astype(BF16), w_mem_kv[i][:, X_WIDTH:].astype(BF16)],
                            [BF16, BF16], tm=min(256, b * n_mem))
        mk = mk.reshape(b, n_mem, X_WIDTH)
        mv = jnp.swapaxes(mv.reshape(b, n_mem, X_WIDTH), 1, 2)
        if i % 2 == 0:
            w_list, wqt, wk, wvt = _pack_mla_weights(w_in_mla[j], mla_w_uq[j], mla_w_ukv[j])
            cq, ckv, kr, qm, gate = _norm_proj(x.reshape(t, d), norm_g[i], w_list, [F32, F32, F32, BF16, BF16],
                                               tm=tm)
            sh = lambda a: a.reshape(b, s, -1)
            qt, k, vt = _mla_up(sh(cq), sh(ckv), sh(kr), posc, posr, row(mla_q_norm_g[j]), row(mla_kv_norm_g[j]),
                                wqt, wk, wvt, invr, invc, tm=min(MLA_UP_TILE, FLASH_TILE, s),
                                tq=min(FLASH_TILE, s))
            mix = _flash(qt, k, vt, groups=FLASH_GROUPS, heads=FLASH_HEADS)
        else:
            w = w_in_rwkv[j]
            edges = np.cumsum([MIX_WIDTH, MIX_WIDTH, MIX_WIDTH, DECAY_LORA + ICLR_LORA, X_WIDTH]).tolist()
            w_list = [a.astype(BF16) for a in jnp.split(w, edges, axis=1)]
            mus = [row(a) for a in jnp.split(rwkv_mu[j], edges[:3])]
            zeros = jnp.zeros((DECAY_LORA, MIX_WIDTH), F32)
            wl = jnp.concatenate([jnp.concatenate([rwkv_w2[j], zeros], axis=1),
                                  jnp.concatenate([zeros, rwkv_a2[j]], axis=1)], axis=0).astype(BF16)
            r, g, kmod, vv, kk, bb, bonus, qm, gate = _rwkv_in(
                x, norm_g[i], w_list, mus, row(rwkv_w0[j]), row(rwkv_a0[j]), wl,
                row(rwkv_k_k[j]), row(rwkv_k_a[j]), row(rwkv_r_k[j]), lmat, _head_seg(SLAB), tm=rwkv_tm)
            gnw = rwkv_gn_w[j].reshape(N_PAIRS, 1, LANES)
            gnb = rwkv_gn_b[j].reshape(N_PAIRS, 1, LANES)
            mix = _rwkv_scan(r, g, kmod, vv, kk, bb, bonus, gnw, gnb, segm)
        x = _out(x, mix, qm.reshape(b, s, -1), gate.reshape(b, s, -1), mk, mv, w_out[i].astype(BF16),
                 ts=min(512, s), final_g=final_g if i == depth - 1 else None)
    return x
```
